```python
import jax, jax.numpy as jnp
from jax import lax
import numpy as np

D_MODEL = 2048
BATCH = 8
SEQ = 4096
DEPTH = 1
DEC_BATCH = 2
DEC_SEQ = 16384
PAST_LEN = 128

GRID_W = 64
HEAD_DIM = 128
ATTN_WIDTH = D_MODEL // 2
N_Q_HEADS = ATTN_WIDTH // HEAD_DIM
N_KV_HEADS = N_Q_HEADS // 4
KV_WIDTH = N_KV_HEADS * HEAD_DIM
RNN_WIDTH = D_MODEL // 2
RNN_BLOCKS = 8
RNN_BLOCK_W = RNN_WIDTH // RNN_BLOCKS
MIX_WIDTH = ATTN_WIDTH + RNN_WIDTH
IN_WIDTH = ATTN_WIDTH + 2 * KV_WIDTH + 2 * RNN_WIDTH
SPLITS = (ATTN_WIDTH, ATTN_WIDTH + KV_WIDTH, ATTN_WIDTH + 2 * KV_WIDTH,
          ATTN_WIDTH + 2 * KV_WIDTH + RNN_WIDTH)
CONV_W = 4
CONV_PAD_L = 2
CONV_PAD_R = 1
RG_C = 8.0
D_FF = 5632
Q_BLOCK = 128
ROPE_THETA = 10000.0
ROPE_PAIRS = HEAD_DIM // 4
EPS = 1e-6

kernel_name = "hymba_attn_rglru_macaron_encoder"


def rms_norm(x, g):
    xf = x.astype(jnp.float32)
    y = xf * lax.rsqrt(jnp.mean(xf * xf, axis=-1, keepdims=True) + EPS)
    return (y * g.astype(jnp.float32)).astype(x.dtype)


def swiglu(h, w_gate, w_up, w_down):
    return (jax.nn.silu(h @ w_gate) * (h @ w_up)) @ w_down


def axial_rope_tables(seq_len):
    rows = seq_len // GRID_W
    row_id = jnp.repeat(jnp.arange(rows, dtype=jnp.float32), GRID_W)
    col_id = jnp.tile(jnp.arange(GRID_W, dtype=jnp.float32), rows)
    inv_freq = ROPE_THETA ** (-jnp.arange(ROPE_PAIRS, dtype=jnp.float32) / ROPE_PAIRS)
    ang = jnp.stack([row_id[:, None] * inv_freq, col_id[:, None] * inv_freq], axis=1)
    return jnp.cos(ang), jnp.sin(ang)


def apply_axial_rope(x, cos, sin):
    B, S, H, _ = x.shape
    xr = x.astype(jnp.float32).reshape(B, S, H, 2, 2, ROPE_PAIRS)
    x1 = xr[..., 0, :]
    x2 = xr[..., 1, :]
    c = cos[None, :, None]
    s = sin[None, :, None]
    out = jnp.stack([x1 * c - x2 * s, x2 * c + x1 * s], axis=-2)
    return out.reshape(B, S, H, HEAD_DIM).astype(x.dtype)


def blocked_attention(q, k, v):
    B, S, _, _ = q.shape
    n_blk = S // Q_BLOCK
    grp = N_Q_HEADS // N_KV_HEADS
    qb = q.reshape(B, n_blk, Q_BLOCK, N_KV_HEADS, grp, HEAD_DIM).transpose(1, 0, 2, 3, 4, 5)
    scale = HEAD_DIM ** -0.5

    def one_block(q_blk):
        s = jnp.einsum('bqkgd,bskd->bkgqs', q_blk, k,
                       preferred_element_type=jnp.float32) * scale
        p = jax.nn.softmax(s, axis=-1)
        return jnp.einsum('bkgqs,bskd->bqkgd', p.astype(v.dtype), v)

    o = lax.map(one_block, qb)
    return o.transpose(1, 0, 2, 3, 4, 5).reshape(B, S, ATTN_WIDTH)


def centred_depthwise_conv(x, w, b):
    S = x.shape[1]
    xp = jnp.pad(x, ((0, 0), (CONV_PAD_L, CONV_PAD_R), (0, 0)))
    y = xp[:, 0:S] * w[0]
    for j in range(1, CONV_W):
        y = y + xp[:, j:j + S] * w[j]
    return y + b


def block_diag_linear(x, w, b):
    B, S, R = x.shape
    xb = x.reshape(B, S, RNN_BLOCKS, RNN_BLOCK_W)
    return jnp.einsum('bsnc,ncd->bsnd', xb, w).reshape(B, S, R) + b


def rg_lru(x, w_a, b_a, w_x, b_x, lam, reverse):
    r = jax.nn.sigmoid(block_diag_linear(x, w_a, b_a).astype(jnp.float32))
    i = jax.nn.sigmoid(block_diag_linear(x, w_x, b_x).astype(jnp.float32))
    log_a = -RG_C * r * jax.nn.softplus(-lam.astype(jnp.float32))
    a = jnp.exp(log_a)
    u = jnp.sqrt(-jnp.expm1(2.0 * log_a)) * (i * x.astype(jnp.float32))

    def combine(left, right):
        a_l, b_l = left
        a_r, b_r = right
        return a_l * a_r, a_r * b_l + b_r

    _, h = lax.associative_scan(combine, (a, u), reverse=reverse, axis=1)
    return h


def encoder_layer(x, g_ffn1_pre, w_ffn1_gate, w_ffn1_up, w_ffn1_down, g_ffn1_post,
                  g_mix_pre, w_in, g_q, g_k, w_conv, b_conv,
                  w_a_fwd, b_a_fwd, w_x_fwd, b_x_fwd, lam_fwd,
                  w_a_bwd, b_a_bwd, w_x_bwd, b_x_bwd, lam_bwd,
                  g_attn_out, g_rnn_out, w_out, g_mix_post,
                  g_ffn2_pre, w_ffn2_gate, w_ffn2_up, w_ffn2_down, g_ffn2_post):
    B, S, _ = x.shape
    h = rms_norm(x, g_ffn1_pre)
    x = x + 0.5 * rms_norm(swiglu(h, w_ffn1_gate, w_ffn1_up, w_ffn1_down), g_ffn1_post)

    h = rms_norm(x, g_mix_pre)
    z = h @ w_in
    q, k, v, xr, yr = jnp.split(z, SPLITS, axis=-1)

    cos, sin = axial_rope_tables(S)
    q = apply_axial_rope(rms_norm(q.reshape(B, S, N_Q_HEADS, HEAD_DIM), g_q), cos, sin)
    k = apply_axial_rope(rms_norm(k.reshape(B, S, N_KV_HEADS, HEAD_DIM), g_k), cos, sin)
    v = v.reshape(B, S, N_KV_HEADS, HEAD_DIM)
    attn = blocked_attention(q, k, v)

    xc = centred_depthwise_conv(xr, w_conv, b_conv)
    h_f = rg_lru(xc, w_a_fwd, b_a_fwd, w_x_fwd, b_x_fwd, lam_fwd, reverse=False)
    h_b = rg_lru(xc, w_a_bwd, b_a_bwd, w_x_bwd, b_x_bwd, lam_bwd, reverse=True)
    rnn = ((h_f + h_b) * jax.nn.gelu(yr.astype(jnp.float32))).astype(x.dtype)

    mixed = jnp.concatenate([rms_norm(attn, g_attn_out), rms_norm(rnn, g_rnn_out)], axis=-1)
    x = x + rms_norm(mixed @ w_out, g_mix_post)

    h = rms_norm(x, g_ffn2_pre)
    x = x + 0.5 * rms_norm(swiglu(h, w_ffn2_gate, w_ffn2_up, w_ffn2_down), g_ffn2_post)
    return x


def setup_inputs(seed: int = 0) -> dict:
    key = jax.random.key(seed)
    ks = jax.random.split(key, 40)
    f32 = jnp.float32

    def nrm(k, shape, fan_in):
        return jax.random.normal(k, shape, f32) * (fan_in ** -0.5)

    def gain(k, n):
        return 1.0 + 0.02 * jax.random.normal(k, (DEPTH, n), f32)

    def bias(k, n):
        return 0.02 * jax.random.normal(k, (DEPTH, n), f32)

    def lam(k):
        u = jax.random.uniform(k, (DEPTH, RNN_WIDTH), f32, minval=0.9, maxval=0.999)
        s = u ** (1.0 / RG_C)
        return jnp.log(s) - jnp.log1p(-s)

    bd = (DEPTH, RNN_BLOCKS, RNN_BLOCK_W, RNN_BLOCK_W)
    return {
        "x_prompt": jax.random.normal(ks[0], (BATCH, SEQ, D_MODEL), f32),
        "x_sample": jax.random.normal(ks[1], (DEC_BATCH, DEC_SEQ, D_MODEL), f32),
        "g_ffn1_pre": gain(ks[2], D_MODEL),
        "w_ffn1_gate": nrm(ks[3], (DEPTH, D_MODEL, D_FF), D_MODEL),
        "w_ffn1_up": nrm(ks[4], (DEPTH, D_MODEL, D_FF), D_MODEL),
        "w_ffn1_down": nrm(ks[5], (DEPTH, D_FF, D_MODEL), D_FF),
        "g_ffn1_post": gain(ks[6], D_MODEL),
        "g_mix_pre": gain(ks[7], D_MODEL),
        "w_in": nrm(ks[8], (DEPTH, D_MODEL, IN_WIDTH), D_MODEL),
        "g_q": gain(ks[9], HEAD_DIM),
        "g_k": gain(ks[10], HEAD_DIM),
        "w_conv": nrm(ks[11], (DEPTH, CONV_W, RNN_WIDTH), CONV_W),
        "b_conv": bias(ks[12], RNN_WIDTH),
        "w_a_fwd": nrm(ks[13], bd, RNN_BLOCK_W),
        "b_a_fwd": bias(ks[14], RNN_WIDTH),
        "w_x_fwd": nrm(ks[15], bd, RNN_BLOCK_W),
        "b_x_fwd": bias(ks[16], RNN_WIDTH),
        "lam_fwd": lam(ks[17]),
        "w_a_bwd": nrm(ks[18], bd, RNN_BLOCK_W),
        "b_a_bwd": bias(ks[19], RNN_WIDTH),
        "w_x_bwd": nrm(ks[20], bd, RNN_BLOCK_W),
        "b_x_bwd": bias(ks[21], RNN_WIDTH),
        "lam_bwd": lam(ks[22]),
        "g_attn_out": gain(ks[23], ATTN_WIDTH),
        "g_rnn_out": gain(ks[24], RNN_WIDTH),
        "w_out": nrm(ks[25], (DEPTH, MIX_WIDTH, D_MODEL), MIX_WIDTH),
        "g_mix_post": gain(ks[26], D_MODEL),
        "g_ffn2_pre": gain(ks[27], D_MODEL),
        "w_ffn2_gate": nrm(ks[28], (DEPTH, D_MODEL, D_FF), D_MODEL),
        "w_ffn2_up": nrm(ks[29], (DEPTH, D_MODEL, D_FF), D_MODEL),
        "w_ffn2_down": nrm(ks[30], (DEPTH, D_FF, D_MODEL), D_FF),
        "g_ffn2_post": gain(ks[31], D_MODEL),
    }


def reference(x_prompt, x_sample, g_ffn1_pre, w_ffn1_gate, w_ffn1_up, w_ffn1_down, g_ffn1_post,
              g_mix_pre, w_in, g_q, g_k, w_conv, b_conv,
              w_a_fwd, b_a_fwd, w_x_fwd, b_x_fwd, lam_fwd,
              w_a_bwd, b_a_bwd, w_x_bwd, b_x_bwd, lam_bwd,
              g_attn_out, g_rnn_out, w_out, g_mix_post,
              g_ffn2_pre, w_ffn2_gate, w_ffn2_up, w_ffn2_down, g_ffn2_post):
    params = (g_ffn1_pre, w_ffn1_gate, w_ffn1_up, w_ffn1_down, g_ffn1_post,
              g_mix_pre, w_in, g_q, g_k, w_conv, b_conv,
              w_a_fwd, b_a_fwd, w_x_fwd, b_x_fwd, lam_fwd,
              w_a_bwd, b_a_bwd, w_x_bwd, b_x_bwd, lam_bwd,
              g_attn_out, g_rnn_out, w_out, g_mix_post,
              g_ffn2_pre, w_ffn2_gate, w_ffn2_up, w_ffn2_down, g_ffn2_post)

    def trunk(x):
        for layer in range(DEPTH):
            x = encoder_layer(x, *[p[layer] for p in params])
        return x

    y_prompt = trunk(x_prompt)
    y_sample = trunk(x_sample)
    return (y_prompt, y_sample)
```

```python
import functools
import math

import jax
import jax.numpy as jnp
from jax import lax
from jax.experimental import pallas as pl
from jax.experimental.pallas import tpu as pltpu

F32 = jnp.float32
BF16 = jnp.bfloat16

EPS = 1e-6
HEAD_DIM = 128
N_Q_HEADS = 8
N_KV_HEADS = 2
Q_PER_KV = N_Q_HEADS // N_KV_HEADS
ATTN_WIDTH = N_Q_HEADS * HEAD_DIM
KV_WIDTH = N_KV_HEADS * HEAD_DIM
RNN_BLOCKS = 8
RNN_BLOCK_W = 128
RNN_WIDTH = RNN_BLOCKS * RNN_BLOCK_W
GRID_W = 64
ROPE_PAIRS = HEAD_DIM // 4
ROPE_THETA = 10000.0
CONV_W = 4
CONV_PAD_L = 2
RG_C = 8.0
LOG2E = math.log2(math.e)
SUBLANES = 8

VMEM_LIMIT_BYTES = 56 * 1024 * 1024


def _tile(n, want):
    t = min(n, want)
    assert n % t == 0, (n, want)
    return t


def _rms(x, g):
    ms = jnp.mean(x * x, axis=-1, keepdims=True)
    return x * lax.rsqrt(ms + EPS) * g


def _sigmoid(x):
    return 1.0 / (1.0 + jnp.exp(-x))


def _resident(shape):
    nd = len(shape)
    return pl.BlockSpec(shape, lambda *_: (0,) * nd, pipeline_mode=pl.Buffered(1))


def _ffn_kernel(x_ref, gpre_ref, wg_ref, wu_ref, wd_ref, gpost_ref, o_ref, h_ref, acc_ref):
    j = pl.program_id(1)
    last = pl.num_programs(1) - 1

    @pl.when(j == 0)
    def _():
        h_ref[...] = _rms(x_ref[...], gpre_ref[...]).astype(BF16)

    h = h_ref[...]
    g = jnp.dot(h, wg_ref[...], preferred_element_type=F32)
    u = jnp.dot(h, wu_ref[...], preferred_element_type=F32)
    a = (g * _sigmoid(g) * u).astype(BF16)
    d = jnp.dot(a, wd_ref[...], preferred_element_type=F32)

    @pl.when(j == 0)
    def _():
        acc_ref[...] = d

    @pl.when(j > 0)
    def _():
        acc_ref[...] += d

    @pl.when(j == last)
    def _():
        o_ref[...] = x_ref[...] + 0.5 * _rms(acc_ref[...], gpost_ref[...])


def _ffn(x, g_pre, wg, wu, wd, g_post, *, tm=512, tf=512):
    t, d = x.shape
    f = wg.shape[1]
    tm = _tile(t, tm)
    tf = _tile(f, tf)
    return pl.pallas_call(
        _ffn_kernel,
        out_shape=jax.ShapeDtypeStruct((t, d), F32),
        grid=(t // tm, f // tf),
        in_specs=[
            pl.BlockSpec((tm, d), lambda i, j: (i, 0)),
            pl.BlockSpec((1, d), lambda i, j: (0, 0)),
            pl.BlockSpec((d, tf), lambda i, j: (0, j)),
            pl.BlockSpec((d, tf), lambda i, j: (0, j)),
            pl.BlockSpec((tf, d), lambda i, j: (j, 0)),
            pl.BlockSpec((1, d), lambda i, j: (0, 0)),
        ],
        out_specs=pl.BlockSpec((tm, d), lambda i, j: (i, 0)),
        scratch_shapes=[pltpu.VMEM((tm, d), BF16), pltpu.VMEM((tm, d), F32)],
        compiler_params=pltpu.CompilerParams(
            dimension_semantics=("parallel", "arbitrary"),
            vmem_limit_bytes=VMEM_LIMIT_BYTES),
        name="ffn",
    )(x, g_pre, wg, wu, wd, g_post)


def _rope(xn, cos, sin_lo, sin_hi):
    return (xn * cos
            + pltpu.roll(xn, HEAD_DIM - ROPE_PAIRS, axis=1) * sin_lo
            + pltpu.roll(xn, ROPE_PAIRS, axis=1) * sin_hi)


def _in_proj_kernel(x_ref, g_ref, wq_ref, wk_ref, wv_ref, wx_ref, wy_ref, gq_ref, gk_ref,
                    cos_ref, slo_ref, shi_ref,
                    q_ref, kt_ref, v_ref, xr_ref, yr_ref):
    h = _rms(x_ref[0], g_ref[...]).astype(BF16)
    cos = cos_ref[...]
    slo = slo_ref[...]
    shi = shi_ref[...]

    q = jnp.dot(h, wq_ref[...], preferred_element_type=F32)
    gq = gq_ref[...] * (HEAD_DIM ** -0.5 * LOG2E)
    for hd in range(N_Q_HEADS):
        sl = slice(hd * HEAD_DIM, (hd + 1) * HEAD_DIM)
        q_ref[0, :, sl] = _rope(_rms(q[:, sl], gq), cos, slo, shi).astype(BF16)

    k = jnp.dot(h, wk_ref[...], preferred_element_type=F32)
    for hd in range(N_KV_HEADS):
        sl = slice(hd * HEAD_DIM, (hd + 1) * HEAD_DIM)
        kr = _rope(_rms(k[:, sl], gk_ref[...]), cos, slo, shi)
        kt_ref[0, hd] = kr.T.astype(BF16)

    v_ref[0] = jnp.dot(h, wv_ref[...], preferred_element_type=F32).astype(BF16)
    xr_ref[0] = jnp.dot(h, wx_ref[...], preferred_element_type=F32)
    yr_ref[0] = jnp.dot(h, wy_ref[...], preferred_element_type=F32)


def _in_proj(x, g, wq, wk, wv, wx, wy, gq, gk, cos, slo, shi, *, tm=512):
    b, s, d = x.shape
    tm = _tile(s, tm)
    tok = lambda w: pl.BlockSpec((1, tm, w), lambda i, j: (i, j, 0))
    tab = pl.BlockSpec((tm, HEAD_DIM), lambda i, j: (j, 0))
    return pl.pallas_call(
        _in_proj_kernel,
        out_shape=(
            jax.ShapeDtypeStruct((b, s, ATTN_WIDTH), BF16),
            jax.ShapeDtypeStruct((b, N_KV_HEADS, HEAD_DIM, s), BF16),
            jax.ShapeDtypeStruct((b, s, KV_WIDTH), BF16),
            jax.ShapeDtypeStruct((b, s, RNN_WIDTH), F32),
            jax.ShapeDtypeStruct((b, s, RNN_WIDTH), F32),
        ),
        grid=(b, s // tm),
        in_specs=[
            tok(d), _resident(g.shape),
            _resident(wq.shape), _resident(wk.shape), _resident(wv.shape),
            _resident(wx.shape), _resident(wy.shape),
            _resident(gq.shape), _resident(gk.shape),
            tab, tab, tab,
        ],
        out_specs=(
            tok(ATTN_WIDTH),
            pl.BlockSpec((1, N_KV_HEADS, HEAD_DIM, tm), lambda i, j: (i, 0, 0, j)),
            tok(KV_WIDTH), tok(RNN_WIDTH), tok(RNN_WIDTH),
        ),
        compiler_params=pltpu.CompilerParams(
            dimension_semantics=("parallel", "parallel"),
            vmem_limit_bytes=VMEM_LIMIT_BYTES),
        name="in_proj",
    )(x, g, wq, wk, wv, wx, wy, gq, gk, cos, slo, shi)


def _attn_kernel(q_ref, kt_ref, v_ref, o_ref, *, tk):
    tq = q_ref.shape[1]
    n_chunks = kt_ref.shape[3] // tk

    for hd in range(Q_PER_KV):
        sl = slice(hd * HEAD_DIM, (hd + 1) * HEAD_DIM)
        q = q_ref[0, :, sl]

        def chunk(c, carry):
            m, l, acc = carry
            start = pl.multiple_of(c * tk, tk)
            s = jnp.dot(q, kt_ref[0, 0, :, pl.ds(start, tk)], preferred_element_type=F32)
            m_new = jnp.maximum(m, jnp.max(s, axis=-1, keepdims=True))
            p = jnp.exp2(s - m_new)
            alpha = jnp.exp2(m - m_new)
            l = alpha * l + jnp.sum(p, axis=-1, keepdims=True)
            acc = alpha * acc + jnp.dot(p.astype(BF16), v_ref[0, pl.ds(start, tk), :],
                                        preferred_element_type=F32)
            return m_new, l, acc

        init = (jnp.full((tq, 1), -jnp.inf, F32), jnp.zeros((tq, 1), F32),
                jnp.zeros((tq, HEAD_DIM), F32))
        _, l, acc = lax.fori_loop(0, n_chunks, chunk, init)
        o_ref[0, :, sl] = acc / l


def _attention(q, kt, v, *, tq=512, tk=2048):
    b, s, _ = q.shape
    tq = _tile(s, tq)
    tk = _tile(s, tk)
    gw = Q_PER_KV * HEAD_DIM
    return pl.pallas_call(
        functools.partial(_attn_kernel, tk=tk),
        out_shape=jax.ShapeDtypeStruct((b, s, ATTN_WIDTH), F32),
        grid=(b, N_KV_HEADS, s // tq),
        in_specs=[
            pl.BlockSpec((1, tq, gw), lambda i, g, j: (i, j, g)),
            pl.BlockSpec((1, 1, HEAD_DIM, s), lambda i, g, j: (i, g, 0, 0)),
            pl.BlockSpec((1, s, HEAD_DIM), lambda i, g, j: (i, 0, g)),
        ],
        out_specs=pl.BlockSpec((1, tq, gw), lambda i, g, j: (i, j, g)),
        compiler_params=pltpu.CompilerParams(
            dimension_semantics=("parallel", "parallel", "arbitrary"),
            vmem_limit_bytes=VMEM_LIMIT_BYTES),
        name="attention",
    )(q, kt, v)


def _gates(x_ref, prev_ref, next_ref, wc_ref, bc_ref, wg_ref, ba_ref, bx_ref, lam_ref,
           ext_ref, a_ref, u_ref, *, reverse):
    nt = pl.num_programs(1)
    t = nt - 1 - pl.program_id(1) if reverse else pl.program_id(1)
    tt = x_ref.shape[1]
    ext_ref[0:SUBLANES] = jnp.where(t > 0, prev_ref[0], 0.0)
    ext_ref[SUBLANES:SUBLANES + tt] = x_ref[0]
    ext_ref[SUBLANES + tt:] = jnp.where(t < nt - 1, next_ref[0], 0.0)
    nl = -lam_ref[...]
    decay = -RG_C * (jnp.maximum(nl, 0.0) + jnp.log(1.0 + jnp.exp(-jnp.abs(nl))))
    for n in range(RNN_BLOCKS):
        sl = slice(n * RNN_BLOCK_W, (n + 1) * RNN_BLOCK_W)
        xc = bc_ref[:, sl]
        for j in range(CONV_W):
            off = SUBLANES - CONV_PAD_L + j
            xc = xc + ext_ref[off:off + tt, sl] * wc_ref[j:j + 1, sl]
        gates = jnp.dot(xc.astype(BF16), wg_ref[n], preferred_element_type=F32)
        r = _sigmoid(gates[:, :RNN_BLOCK_W] + ba_ref[:, sl])
        i = _sigmoid(gates[:, RNN_BLOCK_W:] + bx_ref[:, sl])
        a = jnp.exp(decay[:, sl] * r)
        a_ref[:, sl] = a
        u_ref[:, sl] = jnp.sqrt(1.0 - a * a) * (i * xc)


def _rglru_fwd_kernel(x_ref, prev_ref, next_ref, wc_ref, bc_ref, wg_ref, ba_ref, bx_ref, lam_ref,
                      hf_ref, ext_ref, a_ref, u_ref, carry_ref):
    tt = x_ref.shape[1]

    @pl.when(pl.program_id(1) == 0)
    def _():
        carry_ref[...] = jnp.zeros_like(carry_ref)

    _gates(x_ref, prev_ref, next_ref, wc_ref, bc_ref, wg_ref, ba_ref, bx_ref, lam_ref,
           ext_ref, a_ref, u_ref, reverse=False)

    def step(i, h):
        h = a_ref[pl.ds(i, 1), :] * h + u_ref[pl.ds(i, 1), :]
        hf_ref[0, pl.ds(i, 1), :] = h
        return h

    carry_ref[...] = lax.fori_loop(0, tt, step, carry_ref[...], unroll=8)


def _rglru_bwd_kernel(x_ref, prev_ref, next_ref, wc_ref, bc_ref, wg_ref, ba_ref, bx_ref, lam_ref,
                      hf_ref, y_ref, o_ref, ext_ref, a_ref, u_ref, carry_ref):
    tt = x_ref.shape[1]

    @pl.when(pl.program_id(1) == 0)
    def _():
        carry_ref[...] = jnp.zeros_like(carry_ref)

    _gates(x_ref, prev_ref, next_ref, wc_ref, bc_ref, wg_ref, ba_ref, bx_ref, lam_ref,
           ext_ref, a_ref, u_ref, reverse=True)

    def step(k, h):
        i = tt - 1 - k
        h = a_ref[pl.ds(i, 1), :] * h + u_ref[pl.ds(i, 1), :]
        u_ref[pl.ds(i, 1), :] = h
        return h

    carry_ref[...] = lax.fori_loop(0, tt, step, carry_ref[...], unroll=8)

    y = y_ref[0]
    gelu = 0.5 * y * (1.0 + jnp.tanh(math.sqrt(2.0 / math.pi) * (y + 0.044715 * (y * y * y))))
    o_ref[0] = (hf_ref[0] + u_ref[...]) * gelu


def _rglru(xr, yr, wc, bc, wg_f, ba_f, bx_f, lam_f, wg_b, ba_b, bx_b, lam_b, *, tt=512):
    b, s, r = xr.shape
    tt = _tile(s, tt)
    nt = s // tt
    groups = tt // SUBLANES
    last_group = s // SUBLANES - 1
    params = [_resident(wc.shape), _resident(bc.shape), _resident(wg_f.shape),
              _resident(ba_f.shape), _resident(bx_f.shape), _resident(lam_f.shape)]
    scratch = [pltpu.VMEM((tt + 2 * SUBLANES, r), F32), pltpu.VMEM((tt, r), F32),
               pltpu.VMEM((tt, r), F32), pltpu.VMEM((1, r), F32)]
    cparams = pltpu.CompilerParams(dimension_semantics=("parallel", "arbitrary"),
                                   vmem_limit_bytes=VMEM_LIMIT_BYTES)

    def specs(time_of):
        cur = pl.BlockSpec((1, tt, r), lambda i, j: (i, time_of(j), 0))
        prev = pl.BlockSpec((1, SUBLANES, r),
                            lambda i, j: (i, jnp.maximum(time_of(j) * groups - 1, 0), 0))
        nxt = pl.BlockSpec((1, SUBLANES, r),
                           lambda i, j: (i, jnp.minimum((time_of(j) + 1) * groups, last_group), 0))
        return cur, prev, nxt

    cur, prev, nxt = specs(lambda j: j)
    hf = pl.pallas_call(
        _rglru_fwd_kernel,
        out_shape=jax.ShapeDtypeStruct((b, s, r), F32),
        grid=(b, nt),
        in_specs=[cur, prev, nxt] + params,
        out_specs=cur,
        scratch_shapes=scratch,
        compiler_params=cparams,
        name="rglru_fwd",
    )(xr, xr, xr, wc, bc, wg_f, ba_f, bx_f, lam_f)

    cur, prev, nxt = specs(lambda j: nt - 1 - j)
    return pl.pallas_call(
        _rglru_bwd_kernel,
        out_shape=jax.ShapeDtypeStruct((b, s, r), F32),
        grid=(b, nt),
        in_specs=[cur, prev, nxt] + params + [cur, cur],
        out_specs=cur,
        scratch_shapes=scratch,
        compiler_params=cparams,
        name="rglru_bwd",
    )(xr, xr, xr, wc, bc, wg_b, ba_b, bx_b, lam_b, hf, yr)


def _out_proj_kernel(x_ref, a_ref, r_ref, ga_ref, gr_ref, wa_ref, wr_ref, gp_ref, o_ref):
    an = _rms(a_ref[...], ga_ref[...]).astype(BF16)
    rn = _rms(r_ref[...], gr_ref[...]).astype(BF16)
    y = (jnp.dot(an, wa_ref[...], preferred_element_type=F32)
         + jnp.dot(rn, wr_ref[...], preferred_element_type=F32))
    o_ref[...] = x_ref[...] + _rms(y, gp_ref[...])


def _out_proj(x, attn, rnn, ga, gr, wa, wr, gp, *, tm=512):
    t, d = x.shape
    tm = _tile(t, tm)
    tok = lambda w: pl.BlockSpec((tm, w), lambda i: (i, 0))
    return pl.pallas_call(
        _out_proj_kernel,
        out_shape=jax.ShapeDtypeStruct((t, d), F32),
        grid=(t // tm,),
        in_specs=[tok(d), tok(ATTN_WIDTH), tok(RNN_WIDTH),
                  _resident(ga.shape), _resident(gr.shape),
                  _resident(wa.shape), _resident(wr.shape), _resident(gp.shape)],
        out_specs=tok(d),
        compiler_params=pltpu.CompilerParams(
            dimension_semantics=("parallel",),
            vmem_limit_bytes=VMEM_LIMIT_BYTES),
        name="out_proj",
    )(x, attn, rnn, ga, gr, wa, wr, gp)


def _rope_tables(s):
    pos = jnp.arange(s, dtype=jnp.int32)
    row = (pos // GRID_W).astype(F32)
    col = (pos % GRID_W).astype(F32)
    inv_freq = ROPE_THETA ** (-jnp.arange(ROPE_PAIRS, dtype=F32) / ROPE_PAIRS)
    ang_r = row[:, None] * inv_freq
    ang_c = col[:, None] * inv_freq
    zero = jnp.zeros_like(ang_r)
    cos = jnp.concatenate([jnp.cos(ang_r)] * 2 + [jnp.cos(ang_c)] * 2, axis=1)
    sin_lo = jnp.concatenate([-jnp.sin(ang_r), zero, -jnp.sin(ang_c), zero], axis=1)
    sin_hi = jnp.concatenate([zero, jnp.sin(ang_r), zero, jnp.sin(ang_c)], axis=1)
    return cos, sin_lo, sin_hi


def _prepare(p):
    row = lambda v: v.reshape(1, -1)
    w_in = p["w_in"].astype(BF16)
    c0, c1, c2, c3 = ATTN_WIDTH, ATTN_WIDTH + KV_WIDTH, ATTN_WIDTH + 2 * KV_WIDTH, \
        ATTN_WIDTH + 2 * KV_WIDTH + RNN_WIDTH
    w_out = p["w_out"].astype(BF16)
    out = dict(
        ffn1=(row(p["g_ffn1_pre"]), p["w_ffn1_gate"].astype(BF16), p["w_ffn1_up"].astype(BF16),
              p["w_ffn1_down"].astype(BF16), row(p["g_ffn1_post"])),
        ffn2=(row(p["g_ffn2_pre"]), p["w_ffn2_gate"].astype(BF16), p["w_ffn2_up"].astype(BF16),
              p["w_ffn2_down"].astype(BF16), row(p["g_ffn2_post"])),
        in_proj=(row(p["g_mix_pre"]), w_in[:, :c0], w_in[:, c0:c1], w_in[:, c1:c2],
                 w_in[:, c2:c3], w_in[:, c3:], row(p["g_q"]), row(p["g_k"])),
        rglru=(p["w_conv"], row(p["b_conv"]),
               jnp.concatenate([p["w_a_fwd"], p["w_x_fwd"]], axis=-1).astype(BF16),
               row(p["b_a_fwd"]), row(p["b_x_fwd"]), row(p["lam_fwd"]),
               jnp.concatenate([p["w_a_bwd"], p["w_x_bwd"]], axis=-1).astype(BF16),
               row(p["b_a_bwd"]), row(p["b_x_bwd"]), row(p["lam_bwd"])),
        out_proj=(row(p["g_attn_out"]), row(p["g_rnn_out"]), w_out[:ATTN_WIDTH], w_out[ATTN_WIDTH:],
                  row(p["g_mix_post"])),
    )
    return out


def _layer(x, w):
    b, s, d = x.shape
    x1 = _ffn(x.reshape(b * s, d), *w["ffn1"])
    q, kt, v, xr, yr = _in_proj(x1.reshape(b, s, d), *w["in_proj"], *_rope_tables(s))
    attn = _attention(q, kt, v)
    rnn = _rglru(xr, yr, *w["rglru"])
    x2 = _out_proj(x1, attn.reshape(b * s, -1), rnn.reshape(b * s, -1), *w["out_proj"])
    x3 = _ffn(x2, *w["ffn2"])
    return x3.reshape(b, s, d)


_PARAM_NAMES = (
    "g_ffn1_pre", "w_ffn1_gate", "w_ffn1_up", "w_ffn1_down", "g_ffn1_post",
    "g_mix_pre", "w_in", "g_q", "g_k", "w_conv", "b_conv",
    "w_a_fwd", "b_a_fwd", "w_x_fwd", "b_x_fwd", "lam_fwd",
    "w_a_bwd", "b_a_bwd", "w_x_bwd", "b_x_bwd", "lam_bwd",
    "g_attn_out", "g_rnn_out", "w_out", "g_mix_post",
    "g_ffn2_pre", "w_ffn2_gate", "w_ffn2_up", "w_ffn2_down", "g_ffn2_post")


def kernel(x_prompt, x_sample, g_ffn1_pre, w_ffn1_gate, w_ffn1_up, w_ffn1_down, g_ffn1_post, g_mix_pre, w_in, g_q, g_k, w_conv, b_conv, w_a_fwd, b_a_fwd, w_x_fwd, b_x_fwd, lam_fwd, w_a_bwd, b_a_bwd, w_x_bwd, b_x_bwd, lam_bwd, g_attn_out, g_rnn_out, w_out, g_mix_post, g_ffn2_pre, w_ffn2_gate, w_ffn2_up, w_ffn2_down, g_ffn2_post):
    stacked = (g_ffn1_pre, w_ffn1_gate, w_ffn1_up, w_ffn1_down, g_ffn1_post, g_mix_pre, w_in, g_q,
               g_k, w_conv, b_conv, w_a_fwd, b_a_fwd, w_x_fwd, b_x_fwd, lam_fwd, w_a_bwd, b_a_bwd,
               w_x_bwd, b_x_bwd, lam_bwd, g_attn_out, g_rnn_out, w_out, g_mix_post, g_ffn2_pre,
               w_ffn2_gate, w_ffn2_up, w_ffn2_down, g_ffn2_post)
    depth = g_ffn1_pre.shape[0]
    y_prompt, y_sample = x_prompt, x_sample
    for layer in range(depth):
        w = _prepare({n: a[layer] for n, a in zip(_PARAM_NAMES, stacked)})
        y_prompt = _layer(y_prompt, w)
        y_sample = _layer(y_sample, w)
    return (y_prompt, y_sample)
```

```python
import functools
import math

import jax
import jax.numpy as jnp
from jax import lax
from jax.experimental import pallas as pl
from jax.experimental.pallas import tpu as pltpu

F32 = jnp.float32
BF16 = jnp.bfloat16

EPS = 1e-6
HEAD_DIM = 128
N_Q_HEADS = 8
N_KV_HEADS = 2
Q_PER_KV = N_Q_HEADS // N_KV_HEADS
ATTN_WIDTH = N_Q_HEADS * HEAD_DIM
KV_WIDTH = N_KV_HEADS * HEAD_DIM
RNN_BLOCKS = 8
RNN_BLOCK_W = 128
RNN_WIDTH = RNN_BLOCKS * RNN_BLOCK_W
GRID_W = 64
ROPE_PAIRS = HEAD_DIM // 4
ROPE_THETA = 10000.0
CONV_W = 4
CONV_PAD_L = 2
RG_C = 8.0
LOG2E = math.log2(math.e)
SUBLANES = 8

VMEM_LIMIT_BYTES = 56 * 1024 * 1024


def _tile(n, want):
    t = min(n, want)
    assert n % t == 0, (n, want)
    return t


def _rms(x, g):
    ms = jnp.mean(x * x, axis=-1, keepdims=True)
    return x * lax.rsqrt(ms + EPS) * g


def _sigmoid(x):
    return 1.0 / (1.0 + jnp.exp(-x))


def _resident(shape):
    nd = len(shape)
    return pl.BlockSpec(shape, lambda *_: (0,) * nd, pipeline_mode=pl.Buffered(1))


def _ffn_kernel(x_ref, gpre_ref, wg_ref, wu_ref, wd_ref, gpost_ref, o_ref, h_ref, acc_ref):
    j = pl.program_id(1)
    last = pl.num_programs(1) - 1

    @pl.when(j == 0)
    def _():
        h_ref[...] = _rms(x_ref[...], gpre_ref[...]).astype(BF16)
        acc_ref[...] = jnp.zeros_like(acc_ref)

    h = h_ref[...]
    g = jnp.dot(h, wg_ref[...], preferred_element_type=F32)
    u = jnp.dot(h, wu_ref[...], preferred_element_type=F32)
    a = (g * _sigmoid(g) * u).astype(BF16)
    acc_ref[...] += jnp.dot(a, wd_ref[...], preferred_element_type=F32)

    @pl.when(j == last)
    def _():
        o_ref[...] = x_ref[...] + 0.5 * _rms(acc_ref[...], gpost_ref[...])


def _ffn(x, g_pre, wg, wu, wd, g_post, *, tm=512, tf=512):
    t, d = x.shape
    f = wg.shape[1]
    tm = _tile(t, tm)
    tf = _tile(f, tf)
    return pl.pallas_call(
        _ffn_kernel,
        out_shape=jax.ShapeDtypeStruct((t, d), F32),
        grid=(t // tm, f // tf),
        in_specs=[
            pl.BlockSpec((tm, d), lambda i, j: (i, 0)),
            pl.BlockSpec((1, d), lambda i, j: (0, 0)),
            pl.BlockSpec((d, tf), lambda i, j: (0, j)),
            pl.BlockSpec((d, tf), lambda i, j: (0, j)),
            pl.BlockSpec((tf, d), lambda i, j: (j, 0)),
            pl.BlockSpec((1, d), lambda i, j: (0, 0)),
        ],
        out_specs=pl.BlockSpec((tm, d), lambda i, j: (i, 0)),
        scratch_shapes=[pltpu.VMEM((tm, d), BF16), pltpu.VMEM((tm, d), F32)],
        compiler_params=pltpu.CompilerParams(
            dimension_semantics=("parallel", "arbitrary"),
            vmem_limit_bytes=VMEM_LIMIT_BYTES),
        name="ffn",
    )(x, g_pre, wg, wu, wd, g_post)


def _rope(xn, cos, sin_lo, sin_hi):
    return (xn * cos
            + pltpu.roll(xn, HEAD_DIM - ROPE_PAIRS, axis=1) * sin_lo
            + pltpu.roll(xn, ROPE_PAIRS, axis=1) * sin_hi)


def _in_proj_kernel(x_ref, g_ref, wq_ref, wk_ref, wv_ref, wx_ref, wy_ref, gq_ref, gk_ref,
                    cos_ref, slo_ref, shi_ref,
                    q_ref, kt_ref, v_ref, xr_ref, yr_ref):
    h = _rms(x_ref[0], g_ref[...]).astype(BF16)
    cos = cos_ref[...]
    slo = slo_ref[...]
    shi = shi_ref[...]

    q = jnp.dot(h, wq_ref[...], preferred_element_type=F32)
    gq = gq_ref[...] * (HEAD_DIM ** -0.5 * LOG2E)
    for hd in range(N_Q_HEADS):
        sl = slice(hd * HEAD_DIM, (hd + 1) * HEAD_DIM)
        q_ref[0, :, sl] = _rope(_rms(q[:, sl], gq), cos, slo, shi).astype(BF16)

    k = jnp.dot(h, wk_ref[...], preferred_element_type=F32)
    for hd in range(N_KV_HEADS):
        sl = slice(hd * HEAD_DIM, (hd + 1) * HEAD_DIM)
        kr = _rope(_rms(k[:, sl], gk_ref[...]), cos, slo, shi)
        kt_ref[0, hd] = kr.T.astype(BF16)

    v_ref[0] = jnp.dot(h, wv_ref[...], preferred_element_type=F32).astype(BF16)
    xr_ref[0] = jnp.dot(h, wx_ref[...], preferred_element_type=F32)
    yr_ref[0] = jnp.dot(h, wy_ref[...], preferred_element_type=F32)


def _in_proj(x, g, wq, wk, wv, wx, wy, gq, gk, cos, slo, shi, *, tm=512):
    b, s, d = x.shape
    tm = _tile(s, tm)
    tok = lambda w: pl.BlockSpec((1, tm, w), lambda i, j: (i, j, 0))
    tab = pl.BlockSpec((tm, HEAD_DIM), lambda i, j: (j, 0))
    return pl.pallas_call(
        _in_proj_kernel,
        out_shape=(
            jax.ShapeDtypeStruct((b, s, ATTN_WIDTH), BF16),
            jax.ShapeDtypeStruct((b, N_KV_HEADS, HEAD_DIM, s), BF16),
            jax.ShapeDtypeStruct((b, s, KV_WIDTH), BF16),
            jax.ShapeDtypeStruct((b, s, RNN_WIDTH), F32),
            jax.ShapeDtypeStruct((b, s, RNN_WIDTH), F32),
        ),
        grid=(b, s // tm),
        in_specs=[
            tok(d), _resident(g.shape),
            _resident(wq.shape), _resident(wk.shape), _resident(wv.shape),
            _resident(wx.shape), _resident(wy.shape),
            _resident(gq.shape), _resident(gk.shape),
            tab, tab, tab,
        ],
        out_specs=(
            tok(ATTN_WIDTH),
            pl.BlockSpec((1, N_KV_HEADS, HEAD_DIM, tm), lambda i, j: (i, 0, 0, j)),
            tok(KV_WIDTH), tok(RNN_WIDTH), tok(RNN_WIDTH),
        ),
        compiler_params=pltpu.CompilerParams(
            dimension_semantics=("parallel", "parallel"),
            vmem_limit_bytes=VMEM_LIMIT_BYTES),
        name="in_proj",
    )(x, g, wq, wk, wv, wx, wy, gq, gk, cos, slo, shi)


def _attn_kernel(q_ref, kt_ref, v_ref, o_ref, *, tk):
    tq = q_ref.shape[1]
    n_chunks = kt_ref.shape[3] // tk

    heads = [slice(hd * HEAD_DIM, (hd + 1) * HEAD_DIM) for hd in range(Q_PER_KV)]

    def chunk(c, carry):
        start = pl.multiple_of(c * tk, tk)
        kt = kt_ref[0, 0, :, pl.ds(start, tk)]
        v = v_ref[0, pl.ds(start, tk), :]
        out = []
        for sl, (m, l, acc) in zip(heads, carry):
            s = jnp.dot(q_ref[0, :, sl], kt, preferred_element_type=F32)
            m_new = jnp.maximum(m, jnp.max(s, axis=-1, keepdims=True))
            p = jnp.exp2(s - m_new)
            alpha = jnp.exp2(m - m_new)
            l = alpha * l + jnp.sum(p, axis=-1, keepdims=True)
            acc = alpha * acc + jnp.dot(p.astype(BF16), v, preferred_element_type=F32)
            out.append((m_new, l, acc))
        return tuple(out)

    init = tuple((jnp.full((tq, 1), -jnp.inf, F32), jnp.zeros((tq, 1), F32),
                  jnp.zeros((tq, HEAD_DIM), F32)) for _ in heads)
    final = lax.fori_loop(0, n_chunks, chunk, init)
    for sl, (_, l, acc) in zip(heads, final):
        o_ref[0, :, sl] = acc / l


def _attention(q, kt, v, *, tq=512, tk=2048):
    b, s, _ = q.shape
    tq = _tile(s, tq)
    tk = _tile(s, tk)
    gw = Q_PER_KV * HEAD_DIM
    return pl.pallas_call(
        functools.partial(_attn_kernel, tk=tk),
        out_shape=jax.ShapeDtypeStruct((b, s, ATTN_WIDTH), F32),
        grid=(b, N_KV_HEADS, s // tq),
        in_specs=[
            pl.BlockSpec((1, tq, gw), lambda i, g, j: (i, j, g)),
            pl.BlockSpec((1, 1, HEAD_DIM, s), lambda i, g, j: (i, g, 0, 0)),
            pl.BlockSpec((1, s, HEAD_DIM), lambda i, g, j: (i, 0, g)),
        ],
        out_specs=pl.BlockSpec((1, tq, gw), lambda i, g, j: (i, j, g)),
        compiler_params=pltpu.CompilerParams(
            dimension_semantics=("parallel", "parallel", "arbitrary"),
            vmem_limit_bytes=VMEM_LIMIT_BYTES),
        name="attention",
    )(q, kt, v)


def _gates(x_ref, prev_ref, next_ref, wc_ref, bc_ref, wg_ref, ba_ref, bx_ref, lam_ref,
           ext_ref, a_ref, u_ref, *, reverse):
    nt = pl.num_programs(1)
    t = nt - 1 - pl.program_id(1) if reverse else pl.program_id(1)
    tt = x_ref.shape[1]
    ext_ref[0:SUBLANES] = jnp.where(t > 0, prev_ref[0], 0.0)
    ext_ref[SUBLANES:SUBLANES + tt] = x_ref[0]
    ext_ref[SUBLANES + tt:] = jnp.where(t < nt - 1, next_ref[0], 0.0)
    nl = -lam_ref[...]
    decay = -RG_C * (jnp.maximum(nl, 0.0) + jnp.log(1.0 + jnp.exp(-jnp.abs(nl))))
    for n in range(RNN_BLOCKS):
        sl = slice(n * RNN_BLOCK_W, (n + 1) * RNN_BLOCK_W)
        xc = bc_ref[:, sl]
        for j in range(CONV_W):
            off = SUBLANES - CONV_PAD_L + j
            xc = xc + ext_ref[off:off + tt, sl] * wc_ref[j:j + 1, sl]
        gates = jnp.dot(xc.astype(BF16), wg_ref[n], preferred_element_type=F32)
        r = _sigmoid(gates[:, :RNN_BLOCK_W] + ba_ref[:, sl])
        i = _sigmoid(gates[:, RNN_BLOCK_W:] + bx_ref[:, sl])
        a = jnp.exp(decay[:, sl] * r)
        a_ref[:, sl] = a
        u_ref[:, sl] = jnp.sqrt(1.0 - a * a) * (i * xc)


def _rglru_fwd_kernel(x_ref, prev_ref, next_ref, wc_ref, bc_ref, wg_ref, ba_ref, bx_ref, lam_ref,
                      hf_ref, ext_ref, a_ref, u_ref, carry_ref):
    tt = x_ref.shape[1]

    @pl.when(pl.program_id(1) == 0)
    def _():
        carry_ref[...] = jnp.zeros_like(carry_ref)

    _gates(x_ref, prev_ref, next_ref, wc_ref, bc_ref, wg_ref, ba_ref, bx_ref, lam_ref,
           ext_ref, a_ref, u_ref, reverse=False)

    def step(i, h):
        h = a_ref[pl.ds(i, 1), :] * h + u_ref[pl.ds(i, 1), :]
        hf_ref[0, pl.ds(i, 1), :] = h
        return h

    carry_ref[...] = lax.fori_loop(0, tt, step, carry_ref[...], unroll=8)


def _rglru_bwd_kernel(x_ref, prev_ref, next_ref, wc_ref, bc_ref, wg_ref, ba_ref, bx_ref, lam_ref,
                      hf_ref, y_ref, o_ref, ext_ref, a_ref, u_ref, carry_ref):
    tt = x_ref.shape[1]

    @pl.when(pl.program_id(1) == 0)
    def _():
        carry_ref[...] = jnp.zeros_like(carry_ref)

    _gates(x_ref, prev_ref, next_ref, wc_ref, bc_ref, wg_ref, ba_ref, bx_ref, lam_ref,
           ext_ref, a_ref, u_ref, reverse=True)

    def step(k, h):
        i = tt - 1 - k
        h = a_ref[pl.ds(i, 1), :] * h + u_ref[pl.ds(i, 1), :]
        u_ref[pl.ds(i, 1), :] = h
        return h

    carry_ref[...] = lax.fori_loop(0, tt, step, carry_ref[...], unroll=8)

    y = y_ref[0]
    gelu = 0.5 * y * (1.0 + jnp.tanh(math.sqrt(2.0 / math.pi) * (y + 0.044715 * (y * y * y))))
    o_ref[0] = (hf_ref[0] + u_ref[...]) * gelu


def _rglru(xr, yr, wc, bc, wg_f, ba_f, bx_f, lam_f, wg_b, ba_b, bx_b, lam_b, *, tt=512):
    b, s, r = xr.shape
    tt = _tile(s, tt)
    nt = s // tt
    groups = tt // SUBLANES
    last_group = s // SUBLANES - 1
    params = [_resident(wc.shape), _resident(bc.shape), _resident(wg_f.shape),
              _resident(ba_f.shape), _resident(bx_f.shape), _resident(lam_f.shape)]
    scratch = [pltpu.VMEM((tt + 2 * SUBLANES, r), F32), pltpu.VMEM((tt, r), F32),
               pltpu.VMEM((tt, r), F32), pltpu.VMEM((1, r), F32)]
    cparams = pltpu.CompilerParams(dimension_semantics=("parallel", "arbitrary"),
                                   vmem_limit_bytes=VMEM_LIMIT_BYTES)

    def specs(time_of):
        cur = pl.BlockSpec((1, tt, r), lambda i, j: (i, time_of(j), 0))
        prev = pl.BlockSpec((1, SUBLANES, r),
                            lambda i, j: (i, jnp.maximum(time_of(j) * groups - 1, 0), 0))
        nxt = pl.BlockSpec((1, SUBLANES, r),
                           lambda i, j: (i, jnp.minimum((time_of(j) + 1) * groups, last_group), 0))
        return cur, prev, nxt

    cur, prev, nxt = specs(lambda j: j)
    hf = pl.pallas_call(
        _rglru_fwd_kernel,
        out_shape=jax.ShapeDtypeStruct((b, s, r), F32),
        grid=(b, nt),
        in_specs=[cur, prev, nxt] + params,
        out_specs=cur,
        scratch_shapes=scratch,
        compiler_params=cparams,
        name="rglru_fwd",
    )(xr, xr, xr, wc, bc, wg_f, ba_f, bx_f, lam_f)

    cur, prev, nxt = specs(lambda j: nt - 1 - j)
    return pl.pallas_call(
        _rglru_bwd_kernel,
        out_shape=jax.ShapeDtypeStruct((b, s, r), F32),
        grid=(b, nt),
        in_specs=[cur, prev, nxt] + params + [cur, cur],
        out_specs=cur,
        scratch_shapes=scratch,
        compiler_params=cparams,
        name="rglru_bwd",
    )(xr, xr, xr, wc, bc, wg_b, ba_b, bx_b, lam_b, hf, yr)


def _out_proj_kernel(x_ref, a_ref, r_ref, ga_ref, gr_ref, wa_ref, wr_ref, gp_ref, o_ref):
    an = _rms(a_ref[...], ga_ref[...]).astype(BF16)
    rn = _rms(r_ref[...], gr_ref[...]).astype(BF16)
    y = (jnp.dot(an, wa_ref[...], preferred_element_type=F32)
         + jnp.dot(rn, wr_ref[...], preferred_element_type=F32))
    o_ref[...] = x_ref[...] + _rms(y, gp_ref[...])


def _out_proj(x, attn, rnn, ga, gr, wa, wr, gp, *, tm=512):
    t, d = x.shape
    tm = _tile(t, tm)
    tok = lambda w: pl.BlockSpec((tm, w), lambda i: (i, 0))
    return pl.pallas_call(
        _out_proj_kernel,
        out_shape=jax.ShapeDtypeStruct((t, d), F32),
        grid=(t // tm,),
        in_specs=[tok(d), tok(ATTN_WIDTH), tok(RNN_WIDTH),
                  _resident(ga.shape), _resident(gr.shape),
                  _resident(wa.shape), _resident(wr.shape), _resident(gp.shape)],
        out_specs=tok(d),
        compiler_params=pltpu.CompilerParams(
            dimension_semantics=("parallel",),
            vmem_limit_bytes=VMEM_LIMIT_BYTES),
        name="out_proj",
    )(x, attn, rnn, ga, gr, wa, wr, gp)


def _rope_tables(s):
    pos = jnp.arange(s, dtype=jnp.int32)
    row = (pos // GRID_W).astype(F32)
    col = (pos % GRID_W).astype(F32)
    inv_freq = ROPE_THETA ** (-jnp.arange(ROPE_PAIRS, dtype=F32) / ROPE_PAIRS)
    ang_r = row[:, None] * inv_freq
    ang_c = col[:, None] * inv_freq
    zero = jnp.zeros_like(ang_r)
    cos = jnp.concatenate([jnp.cos(ang_r)] * 2 + [jnp.cos(ang_c)] * 2, axis=1)
    sin_lo = jnp.concatenate([-jnp.sin(ang_r), zero, -jnp.sin(ang_c), zero], axis=1)
    sin_hi = jnp.concatenate([zero, jnp.sin(ang_r), zero, jnp.sin(ang_c)], axis=1)
    return cos, sin_lo, sin_hi


def _prepare(p):
    row = lambda v: v.reshape(1, -1)
    w_in = p["w_in"].astype(BF16)
    c0, c1, c2, c3 = ATTN_WIDTH, ATTN_WIDTH + KV_WIDTH, ATTN_WIDTH + 2 * KV_WIDTH, \
        ATTN_WIDTH + 2 * KV_WIDTH + RNN_WIDTH
    w_out = p["w_out"].astype(BF16)
    out = dict(
        ffn1=(row(p["g_ffn1_pre"]), p["w_ffn1_gate"].astype(BF16), p["w_ffn1_up"].astype(BF16),
              p["w_ffn1_down"].astype(BF16), row(p["g_ffn1_post"])),
        ffn2=(row(p["g_ffn2_pre"]), p["w_ffn2_gate"].astype(BF16), p["w_ffn2_up"].astype(BF16),
              p["w_ffn2_down"].astype(BF16), row(p["g_ffn2_post"])),
        in_proj=(row(p["g_mix_pre"]), w_in[:, :c0], w_in[:, c0:c1], w_in[:, c1:c2],
                 w_in[:, c2:c3], w_in[:, c3:], row(p["g_q"]), row(p["g_k"])),
        rglru=(p["w_conv"], row(p["b_conv"]),
               jnp.concatenate([p["w_a_fwd"], p["w_x_fwd"]], axis=-1).astype(BF16),
               row(p["b_a_fwd"]), row(p["b_x_fwd"]), row(p["lam_fwd"]),
               jnp.concatenate([p["w_a_bwd"], p["w_x_bwd"]], axis=-1).astype(BF16),
               row(p["b_a_bwd"]), row(p["b_x_bwd"]), row(p["lam_bwd"])),
        out_proj=(row(p["g_attn_out"]), row(p["g_rnn_out"]), w_out[:ATTN_WIDTH], w_out[ATTN_WIDTH:],
                  row(p["g_mix_post"])),
    )
    return out


def _layer(x, w):
    b, s, d = x.shape
    x1 = _ffn(x.reshape(b * s, d), *w["ffn1"])
    q, kt, v, xr, yr = _in_proj(x1.reshape(b, s, d), *w["in_proj"], *_rope_tables(s))
    attn = _attention(q, kt, v)
    rnn = _rglru(xr, yr, *w["rglru"])
    x2 = _out_proj(x1, attn.reshape(b * s, -1), rnn.reshape(b * s, -1), *w["out_proj"])
    x3 = _ffn(x2, *w["ffn2"])
    return x3.reshape(b, s, d)


_PARAM_NAMES = (
    "g_ffn1_pre", "w_ffn1_gate", "w_ffn1_up", "w_ffn1_down", "g_ffn1_post",
    "g_mix_pre", "w_in", "g_q", "g_k", "w_conv", "b_conv",
    "w_a_fwd", "b_a_fwd", "w_x_fwd", "b_x_fwd", "lam_fwd",
    "w_a_bwd", "b_a_bwd", "w_x_bwd", "b_x_bwd", "lam_bwd",
    "g_attn_out", "g_rnn_out", "w_out", "g_mix_post",
    "g_ffn2_pre", "w_ffn2_gate", "w_ffn2_up", "w_ffn2_down", "g_ffn2_post")


def kernel(x_prompt, x_sample, g_ffn1_pre, w_ffn1_gate, w_ffn1_up, w_ffn1_down, g_ffn1_post, g_mix_pre, w_in, g_q, g_k, w_conv, b_conv, w_a_fwd, b_a_fwd, w_x_fwd, b_x_fwd, lam_fwd, w_a_bwd, b_a_bwd, w_x_bwd, b_x_bwd, lam_bwd, g_attn_out, g_rnn_out, w_out, g_mix_post, g_ffn2_pre, w_ffn2_gate, w_ffn2_up, w_ffn2_down, g_ffn2_post):
    stacked = (g_ffn1_pre, w_ffn1_gate, w_ffn1_up, w_ffn1_down, g_ffn1_post, g_mix_pre, w_in, g_q,
               g_k, w_conv, b_conv, w_a_fwd, b_a_fwd, w_x_fwd, b_x_fwd, lam_fwd, w_a_bwd, b_a_bwd,
               w_x_bwd, b_x_bwd, lam_bwd, g_attn_out, g_rnn_out, w_out, g_mix_post, g_ffn2_pre,
               w_ffn2_gate, w_ffn2_up, w_ffn2_down, g_ffn2_post)
    depth = g_ffn1_pre.shape[0]
    y_prompt, y_sample = x_prompt, x_sample
    for layer in range(depth):
        w = _prepare({n: a[layer] for n, a in zip(_PARAM_NAMES, stacked)})
        y_prompt = _layer(y_prompt, w)
        y_sample = _layer(y_sample, w)
    return (y_prompt, y_sample)
```

```python
import functools
import math

import jax
import jax.numpy as jnp
from jax import lax
from jax.experimental import pallas as pl
from jax.experimental.pallas import tpu as pltpu

F32 = jnp.float32
BF16 = jnp.bfloat16

EPS = 1e-6
HEAD_DIM = 128
N_Q_HEADS = 8
N_KV_HEADS = 2
Q_PER_KV = N_Q_HEADS // N_KV_HEADS
ATTN_WIDTH = N_Q_HEADS * HEAD_DIM
KV_WIDTH = N_KV_HEADS * HEAD_DIM
RNN_BLOCKS = 8
RNN_BLOCK_W = 128
RNN_WIDTH = RNN_BLOCKS * RNN_BLOCK_W
GRID_W = 64
ROPE_PAIRS = HEAD_DIM // 4
ROPE_THETA = 10000.0
CONV_W = 4
CONV_PAD_L = 2
RG_C = 8.0
LOG2E = math.log2(math.e)
SUBLANES = 8
NORM_ROWS = 32

VMEM_LIMIT_BYTES = 58 * 1024 * 1024


def _tile(n, want):
    t = min(n, want)
    assert n % t == 0, (n, want)
    return t


def _rms(x, g):
    ms = jnp.mean(x * x, axis=-1, keepdims=True)
    return x * lax.rsqrt(ms + EPS) * g


def _sigmoid(x):
    return 1.0 / (1.0 + jnp.exp(-x))


def _resident(shape):
    nd = len(shape)
    return pl.BlockSpec(shape, lambda *_: (0,) * nd, pipeline_mode=pl.Buffered(1))


def _ffn_kernel(x_ref, gpre_ref, wg_ref, wu_ref, wd_ref, gpost_ref, o_ref, h_ref, *, row_split):
    j = pl.program_id(1)
    last = pl.num_programs(1) - 1
    tm, d = x_ref.shape
    n_norm = tm // NORM_ROWS

    @pl.when(j == 0)
    def _():
        def body(i, carry):
            r = pl.ds(pl.multiple_of(i * NORM_ROWS, NORM_ROWS), NORM_ROWS)
            h_ref[r, :] = _rms(x_ref[r, :], gpre_ref[...]).astype(BF16)
            o_ref[r, :] = jnp.zeros((NORM_ROWS, d), F32)
            return carry
        lax.fori_loop(0, n_norm, body, 0, unroll=min(8, n_norm))

    hm = tm // row_split
    for r in range(row_split):
        rows = slice(r * hm, (r + 1) * hm)
        h = h_ref[rows, :]
        g = jnp.dot(h, wg_ref[...], preferred_element_type=F32)
        u = jnp.dot(h, wu_ref[...], preferred_element_type=F32)
        a = (g * _sigmoid(g) * u).astype(BF16)
        o_ref[rows, :] += jnp.dot(a, wd_ref[...], preferred_element_type=F32)

    @pl.when(j == last)
    def _():
        gp = 0.5 * gpost_ref[...]
        for i in range(n_norm):
            r = slice(i * NORM_ROWS, (i + 1) * NORM_ROWS)
            o_ref[r, :] = x_ref[r, :] + _rms(o_ref[r, :], gp)


def _ffn(x, g_pre, wg, wu, wd, g_post, *, tm=1024, tf=512):
    t, d = x.shape
    f = wg.shape[1]
    tm = _tile(t, tm)
    tf = _tile(f, tf)
    row_split = 2 if tm % (2 * NORM_ROWS) == 0 else 1
    return pl.pallas_call(
        functools.partial(_ffn_kernel, row_split=row_split),
        out_shape=jax.ShapeDtypeStruct((t, d), F32),
        grid=(t // tm, f // tf),
        in_specs=[
            pl.BlockSpec((tm, d), lambda i, j: (i, 0)),
            pl.BlockSpec((1, d), lambda i, j: (0, 0)),
            pl.BlockSpec((d, tf), lambda i, j: (0, j)),
            pl.BlockSpec((d, tf), lambda i, j: (0, j)),
            pl.BlockSpec((tf, d), lambda i, j: (j, 0)),
            pl.BlockSpec((1, d), lambda i, j: (0, 0)),
        ],
        out_specs=pl.BlockSpec((tm, d), lambda i, j: (i, 0)),
        scratch_shapes=[pltpu.VMEM((tm, d), BF16)],
        compiler_params=pltpu.CompilerParams(
            dimension_semantics=("parallel", "arbitrary"),
            vmem_limit_bytes=VMEM_LIMIT_BYTES),
        name="ffn",
    )(x, g_pre, wg, wu, wd, g_post)


def _rope(xn, cos, sin_lo, sin_hi):
    return (xn * cos
            + pltpu.roll(xn, HEAD_DIM - ROPE_PAIRS, axis=1) * sin_lo
            + pltpu.roll(xn, ROPE_PAIRS, axis=1) * sin_hi)


def _in_proj_kernel(x_ref, g_ref, wq_ref, wk_ref, wv_ref, wx_ref, wy_ref, gq_ref, gk_ref,
                    cos_ref, slo_ref, shi_ref,
                    q_ref, kt_ref, v_ref, xr_ref, yr_ref):
    h = _rms(x_ref[0], g_ref[...]).astype(BF16)
    cos = cos_ref[...]
    slo = slo_ref[...]
    shi = shi_ref[...]

    q = jnp.dot(h, wq_ref[...], preferred_element_type=F32)
    gq = gq_ref[...] * (HEAD_DIM ** -0.5 * LOG2E)
    for hd in range(N_Q_HEADS):
        sl = slice(hd * HEAD_DIM, (hd + 1) * HEAD_DIM)
        q_ref[0, :, sl] = _rope(_rms(q[:, sl], gq), cos, slo, shi).astype(BF16)

    k = jnp.dot(h, wk_ref[...], preferred_element_type=F32)
    for hd in range(N_KV_HEADS):
        sl = slice(hd * HEAD_DIM, (hd + 1) * HEAD_DIM)
        kr = _rope(_rms(k[:, sl], gk_ref[...]), cos, slo, shi)
        kt_ref[0, hd] = kr.T.astype(BF16)

    v_ref[0] = jnp.dot(h, wv_ref[...], preferred_element_type=F32).astype(BF16)
    xr_ref[0] = jnp.dot(h, wx_ref[...], preferred_element_type=F32)
    yr_ref[0] = jnp.dot(h, wy_ref[...], preferred_element_type=F32)


def _in_proj(x, g, wq, wk, wv, wx, wy, gq, gk, cos, slo, shi, *, tm=512):
    b, s, d = x.shape
    tm = _tile(s, tm)
    tok = lambda w: pl.BlockSpec((1, tm, w), lambda i, j: (i, j, 0))
    tab = pl.BlockSpec((tm, HEAD_DIM), lambda i, j: (j, 0))
    return pl.pallas_call(
        _in_proj_kernel,
        out_shape=(
            jax.ShapeDtypeStruct((b, s, ATTN_WIDTH), BF16),
            jax.ShapeDtypeStruct((b, N_KV_HEADS, HEAD_DIM, s), BF16),
            jax.ShapeDtypeStruct((b, s, KV_WIDTH), BF16),
            jax.ShapeDtypeStruct((b, s, RNN_WIDTH), F32),
            jax.ShapeDtypeStruct((b, s, RNN_WIDTH), F32),
        ),
        grid=(b, s // tm),
        in_specs=[
            tok(d), _resident(g.shape),
            _resident(wq.shape), _resident(wk.shape), _resident(wv.shape),
            _resident(wx.shape), _resident(wy.shape),
            _resident(gq.shape), _resident(gk.shape),
            tab, tab, tab,
        ],
        out_specs=(
            tok(ATTN_WIDTH),
            pl.BlockSpec((1, N_KV_HEADS, HEAD_DIM, tm), lambda i, j: (i, 0, 0, j)),
            tok(KV_WIDTH), tok(RNN_WIDTH), tok(RNN_WIDTH),
        ),
        compiler_params=pltpu.CompilerParams(
            dimension_semantics=("parallel", "parallel"),
            vmem_limit_bytes=VMEM_LIMIT_BYTES),
        name="in_proj",
    )(x, g, wq, wk, wv, wx, wy, gq, gk, cos, slo, shi)


def _attn_kernel(q_ref, kt_ref, v_ref, o_ref, *, tk):
    tq = q_ref.shape[1]
    n_chunks = kt_ref.shape[3] // tk

    heads = [slice(hd * HEAD_DIM, (hd + 1) * HEAD_DIM) for hd in range(Q_PER_KV)]

    def chunk(c, carry):
        start = pl.multiple_of(c * tk, tk)
        kt = kt_ref[0, 0, :, pl.ds(start, tk)]
        v = v_ref[0, pl.ds(start, tk), :]
        out = []
        for sl, (m, l, acc) in zip(heads, carry):
            s = jnp.dot(q_ref[0, :, sl], kt, preferred_element_type=F32)
            m_new = jnp.maximum(m, jnp.max(s, axis=-1, keepdims=True))
            p = jnp.exp2(s - m_new)
            alpha = jnp.exp2(m - m_new)
            l = alpha * l + jnp.sum(p, axis=-1, keepdims=True)
            acc = alpha * acc + jnp.dot(p.astype(BF16), v, preferred_element_type=F32)
            out.append((m_new, l, acc))
        return tuple(out)

    init = tuple((jnp.full((tq, 1), -jnp.inf, F32), jnp.zeros((tq, 1), F32),
                  jnp.zeros((tq, HEAD_DIM), F32)) for _ in heads)
    final = lax.fori_loop(0, n_chunks, chunk, init)
    for sl, (_, l, acc) in zip(heads, final):
        o_ref[0, :, sl] = acc / l


def _attention(q, kt, v, *, tq=1024, tk=2048):
    b, s, _ = q.shape
    tq = _tile(s, tq)
    tk = _tile(s, tk)
    gw = Q_PER_KV * HEAD_DIM
    return pl.pallas_call(
        functools.partial(_attn_kernel, tk=tk),
        out_shape=jax.ShapeDtypeStruct((b, s, ATTN_WIDTH), F32),
        grid=(b, N_KV_HEADS, s // tq),
        in_specs=[
            pl.BlockSpec((1, tq, gw), lambda i, g, j: (i, j, g)),
            pl.BlockSpec((1, 1, HEAD_DIM, s), lambda i, g, j: (i, g, 0, 0),
                         pipeline_mode=pl.Buffered(1)),
            pl.BlockSpec((1, s, HEAD_DIM), lambda i, g, j: (i, 0, g),
                         pipeline_mode=pl.Buffered(1)),
        ],
        out_specs=pl.BlockSpec((1, tq, gw), lambda i, g, j: (i, j, g)),
        compiler_params=pltpu.CompilerParams(
            dimension_semantics=("parallel", "parallel", "arbitrary"),
            vmem_limit_bytes=VMEM_LIMIT_BYTES),
        name="attention",
    )(q, kt, v)


def _gates(x_ref, prev_ref, next_ref, wc_ref, bc_ref, wg_ref, ba_ref, bx_ref, lam_ref,
           ext_ref, a_ref, u_ref, *, reverse):
    nt = pl.num_programs(1)
    t = nt - 1 - pl.program_id(1) if reverse else pl.program_id(1)
    tt = x_ref.shape[1]
    ext_ref[0:SUBLANES] = jnp.where(t > 0, prev_ref[0], 0.0)
    ext_ref[SUBLANES:SUBLANES + tt] = x_ref[0]
    ext_ref[SUBLANES + tt:] = jnp.where(t < nt - 1, next_ref[0], 0.0)
    nl = -lam_ref[...]
    decay = -RG_C * (jnp.maximum(nl, 0.0) + jnp.log(1.0 + jnp.exp(-jnp.abs(nl))))
    for n in range(RNN_BLOCKS):
        sl = slice(n * RNN_BLOCK_W, (n + 1) * RNN_BLOCK_W)
        xc = bc_ref[:, sl]
        for j in range(CONV_W):
            off = SUBLANES - CONV_PAD_L + j
            xc = xc + ext_ref[off:off + tt, sl] * wc_ref[j:j + 1, sl]
        gates = jnp.dot(xc.astype(BF16), wg_ref[n], preferred_element_type=F32)
        r = _sigmoid(gates[:, :RNN_BLOCK_W] + ba_ref[:, sl])
        i = _sigmoid(gates[:, RNN_BLOCK_W:] + bx_ref[:, sl])
        a = jnp.exp(decay[:, sl] * r)
        a_ref[:, sl] = a
        u_ref[:, sl] = jnp.sqrt(1.0 - a * a) * (i * xc)


def _rglru_fwd_kernel(x_ref, prev_ref, next_ref, wc_ref, bc_ref, wg_ref, ba_ref, bx_ref, lam_ref,
                      hf_ref, ext_ref, a_ref, u_ref, carry_ref):
    tt = x_ref.shape[1]

    @pl.when(pl.program_id(1) == 0)
    def _():
        carry_ref[...] = jnp.zeros_like(carry_ref)

    _gates(x_ref, prev_ref, next_ref, wc_ref, bc_ref, wg_ref, ba_ref, bx_ref, lam_ref,
           ext_ref, a_ref, u_ref, reverse=False)

    def step(i, h):
        h = a_ref[pl.ds(i, 1), :] * h + u_ref[pl.ds(i, 1), :]
        hf_ref[0, pl.ds(i, 1), :] = h
        return h

    carry_ref[...] = lax.fori_loop(0, tt, step, carry_ref[...], unroll=8)


def _rglru_bwd_kernel(x_ref, prev_ref, next_ref, wc_ref, bc_ref, wg_ref, ba_ref, bx_ref, lam_ref,
                      hf_ref, y_ref, o_ref, ext_ref, a_ref, u_ref, carry_ref):
    tt = x_ref.shape[1]

    @pl.when(pl.program_id(1) == 0)
    def _():
        carry_ref[...] = jnp.zeros_like(carry_ref)

    _gates(x_ref, prev_ref, next_ref, wc_ref, bc_ref, wg_ref, ba_ref, bx_ref, lam_ref,
           ext_ref, a_ref, u_ref, reverse=True)

    def step(k, h):
        i = tt - 1 - k
        h = a_ref[pl.ds(i, 1), :] * h + u_ref[pl.ds(i, 1), :]
        u_ref[pl.ds(i, 1), :] = h
        return h

    carry_ref[...] = lax.fori_loop(0, tt, step, carry_ref[...], unroll=8)

    y = y_ref[0]
    gelu = 0.5 * y * (1.0 + jnp.tanh(math.sqrt(2.0 / math.pi) * (y + 0.044715 * (y * y * y))))
    o_ref[0] = (hf_ref[0] + u_ref[...]) * gelu


def _rglru(xr, yr, wc, bc, wg_f, ba_f, bx_f, lam_f, wg_b, ba_b, bx_b, lam_b, *, tt=512):
    b, s, r = xr.shape
    tt = _tile(s, tt)
    nt = s // tt
    groups = tt // SUBLANES
    last_group = s // SUBLANES - 1
    params = [_resident(wc.shape), _resident(bc.shape), _resident(wg_f.shape),
              _resident(ba_f.shape), _resident(bx_f.shape), _resident(lam_f.shape)]
    scratch = [pltpu.VMEM((tt + 2 * SUBLANES, r), F32), pltpu.VMEM((tt, r), F32),
               pltpu.VMEM((tt, r), F32), pltpu.VMEM((1, r), F32)]
    cparams = pltpu.CompilerParams(dimension_semantics=("parallel", "arbitrary"),
                                   vmem_limit_bytes=VMEM_LIMIT_BYTES)

    def specs(time_of):
        cur = pl.BlockSpec((1, tt, r), lambda i, j: (i, time_of(j), 0))
        prev = pl.BlockSpec((1, SUBLANES, r),
                            lambda i, j: (i, jnp.maximum(time_of(j) * groups - 1, 0), 0))
        nxt = pl.BlockSpec((1, SUBLANES, r),
                           lambda i, j: (i, jnp.minimum((time_of(j) + 1) * groups, last_group), 0))
        return cur, prev, nxt

    cur, prev, nxt = specs(lambda j: j)
    hf = pl.pallas_call(
        _rglru_fwd_kernel,
        out_shape=jax.ShapeDtypeStruct((b, s, r), F32),
        grid=(b, nt),
        in_specs=[cur, prev, nxt] + params,
        out_specs=cur,
        scratch_shapes=scratch,
        compiler_params=cparams,
        name="rglru_fwd",
    )(xr, xr, xr, wc, bc, wg_f, ba_f, bx_f, lam_f)

    cur, prev, nxt = specs(lambda j: nt - 1 - j)
    return pl.pallas_call(
        _rglru_bwd_kernel,
        out_shape=jax.ShapeDtypeStruct((b, s, r), F32),
        grid=(b, nt),
        in_specs=[cur, prev, nxt] + params + [cur, cur],
        out_specs=cur,
        scratch_shapes=scratch,
        compiler_params=cparams,
        name="rglru_bwd",
    )(xr, xr, xr, wc, bc, wg_b, ba_b, bx_b, lam_b, hf, yr)


def _out_proj_kernel(x_ref, a_ref, r_ref, ga_ref, gr_ref, wa_ref, wr_ref, gp_ref, o_ref):
    an = _rms(a_ref[...], ga_ref[...]).astype(BF16)
    rn = _rms(r_ref[...], gr_ref[...]).astype(BF16)
    y = (jnp.dot(an, wa_ref[...], preferred_element_type=F32)
         + jnp.dot(rn, wr_ref[...], preferred_element_type=F32))
    o_ref[...] = x_ref[...] + _rms(y, gp_ref[...])


def _out_proj(x, attn, rnn, ga, gr, wa, wr, gp, *, tm=512):
    t, d = x.shape
    tm = _tile(t, tm)
    tok = lambda w: pl.BlockSpec((tm, w), lambda i: (i, 0))
    return pl.pallas_call(
        _out_proj_kernel,
        out_shape=jax.ShapeDtypeStruct((t, d), F32),
        grid=(t // tm,),
        in_specs=[tok(d), tok(ATTN_WIDTH), tok(RNN_WIDTH),
                  _resident(ga.shape), _resident(gr.shape),
                  _resident(wa.shape), _resident(wr.shape), _resident(gp.shape)],
        out_specs=tok(d),
        compiler_params=pltpu.CompilerParams(
            dimension_semantics=("parallel",),
            vmem_limit_bytes=VMEM_LIMIT_BYTES),
        name="out_proj",
    )(x, attn, rnn, ga, gr, wa, wr, gp)


def _rope_tables(s):
    pos = jnp.arange(s, dtype=jnp.int32)
    row = (pos // GRID_W).astype(F32)
    col = (pos % GRID_W).astype(F32)
    inv_freq = ROPE_THETA ** (-jnp.arange(ROPE_PAIRS, dtype=F32) / ROPE_PAIRS)
    ang_r = row[:, None] * inv_freq
    ang_c = col[:, None] * inv_freq
    zero = jnp.zeros_like(ang_r)
    cos = jnp.concatenate([jnp.cos(ang_r)] * 2 + [jnp.cos(ang_c)] * 2, axis=1)
    sin_lo = jnp.concatenate([-jnp.sin(ang_r), zero, -jnp.sin(ang_c), zero], axis=1)
    sin_hi = jnp.concatenate([zero, jnp.sin(ang_r), zero, jnp.sin(ang_c)], axis=1)
    return cos, sin_lo, sin_hi


def _prepare(p):
    row = lambda v: v.reshape(1, -1)
    w_in = p["w_in"].astype(BF16)
    c0, c1, c2, c3 = ATTN_WIDTH, ATTN_WIDTH + KV_WIDTH, ATTN_WIDTH + 2 * KV_WIDTH, \
        ATTN_WIDTH + 2 * KV_WIDTH + RNN_WIDTH
    w_out = p["w_out"].astype(BF16)
    out = dict(
        ffn1=(row(p["g_ffn1_pre"]), p["w_ffn1_gate"].astype(BF16), p["w_ffn1_up"].astype(BF16),
              p["w_ffn1_down"].astype(BF16), row(p["g_ffn1_post"])),
        ffn2=(row(p["g_ffn2_pre"]), p["w_ffn2_gate"].astype(BF16), p["w_ffn2_up"].astype(BF16),
              p["w_ffn2_down"].astype(BF16), row(p["g_ffn2_post"])),
        in_proj=(row(p["g_mix_pre"]), w_in[:, :c0], w_in[:, c0:c1], w_in[:, c1:c2],
                 w_in[:, c2:c3], w_in[:, c3:], row(p["g_q"]), row(p["g_k"])),
        rglru=(p["w_conv"], row(p["b_conv"]),
               jnp.concatenate([p["w_a_fwd"], p["w_x_fwd"]], axis=-1).astype(BF16),
               row(p["b_a_fwd"]), row(p["b_x_fwd"]), row(p["lam_fwd"]),
               jnp.concatenate([p["w_a_bwd"], p["w_x_bwd"]], axis=-1).astype(BF16),
               row(p["b_a_bwd"]), row(p["b_x_bwd"]), row(p["lam_bwd"])),
        out_proj=(row(p["g_attn_out"]), row(p["g_rnn_out"]), w_out[:ATTN_WIDTH], w_out[ATTN_WIDTH:],
                  row(p["g_mix_post"])),
    )
    return out


def _layer(x, w):
    b, s, d = x.shape
    x1 = _ffn(x.reshape(b * s, d), *w["ffn1"])
    q, kt, v, xr, yr = _in_proj(x1.reshape(b, s, d), *w["in_proj"], *_rope_tables(s))
    attn = _attention(q, kt, v)
    rnn = _rglru(xr, yr, *w["rglru"])
    x2 = _out_proj(x1, attn.reshape(b * s, -1), rnn.reshape(b * s, -1), *w["out_proj"])
    x3 = _ffn(x2, *w["ffn2"])
    return x3.reshape(b, s, d)


_PARAM_NAMES = (
    "g_ffn1_pre", "w_ffn1_gate", "w_ffn1_up", "w_ffn1_down", "g_ffn1_post",
    "g_mix_pre", "w_in", "g_q", "g_k", "w_conv", "b_conv",
    "w_a_fwd", "b_a_fwd", "w_x_fwd", "b_x_fwd", "lam_fwd",
    "w_a_bwd", "b_a_bwd", "w_x_bwd", "b_x_bwd", "lam_bwd",
    "g_attn_out", "g_rnn_out", "w_out", "g_mix_post",
    "g_ffn2_pre", "w_ffn2_gate", "w_ffn2_up", "w_ffn2_down", "g_ffn2_post")


def kernel(x_prompt, x_sample, g_ffn1_pre, w_ffn1_gate, w_ffn1_up, w_ffn1_down, g_ffn1_post, g_mix_pre, w_in, g_q, g_k, w_conv, b_conv, w_a_fwd, b_a_fwd, w_x_fwd, b_x_fwd, lam_fwd, w_a_bwd, b_a_bwd, w_x_bwd, b_x_bwd, lam_bwd, g_attn_out, g_rnn_out, w_out, g_mix_post, g_ffn2_pre, w_ffn2_gate, w_ffn2_up, w_ffn2_down, g_ffn2_post):
    stacked = (g_ffn1_pre, w_ffn1_gate, w_ffn1_up, w_ffn1_down, g_ffn1_post, g_mix_pre, w_in, g_q,
               g_k, w_conv, b_conv, w_a_fwd, b_a_fwd, w_x_fwd, b_x_fwd, lam_fwd, w_a_bwd, b_a_bwd,
               w_x_bwd, b_x_bwd, lam_bwd, g_attn_out, g_rnn_out, w_out, g_mix_post, g_ffn2_pre,
               w_ffn2_gate, w_ffn2_up, w_ffn2_down, g_ffn2_post)
    depth = g_ffn1_pre.shape[0]
    y_prompt, y_sample = x_prompt, x_sample
    for layer in range(depth):
        w = _prepare({n: a[layer] for n, a in zip(_PARAM_NAMES, stacked)})
        y_prompt = _layer(y_prompt, w)
        y_sample = _layer(y_sample, w)
    return (y_prompt, y_sample)
```

```python
import functools
import math

import jax
import jax.numpy as jnp
from jax import lax
from jax.experimental import pallas as pl
from jax.experimental.pallas import tpu as pltpu

F32 = jnp.float32
BF16 = jnp.bfloat16

EPS = 1e-6
HEAD_DIM = 128
N_Q_HEADS = 8
N_KV_HEADS = 2
Q_PER_KV = N_Q_HEADS // N_KV_HEADS
ATTN_WIDTH = N_Q_HEADS * HEAD_DIM
KV_WIDTH = N_KV_HEADS * HEAD_DIM
RNN_BLOCKS = 8
RNN_BLOCK_W = 128
RNN_WIDTH = RNN_BLOCKS * RNN_BLOCK_W
GRID_W = 64
ROPE_PAIRS = HEAD_DIM // 4
ROPE_THETA = 10000.0
CONV_W = 4
CONV_PAD_L = 2
RG_C = 8.0
LOG2E = math.log2(math.e)
SUBLANES = 8
NORM_ROWS = 32

VMEM_LIMIT_BYTES = 58 * 1024 * 1024


def _tile(n, want):
    t = min(n, want)
    assert n % t == 0, (n, want)
    return t


def _rms(x, g):
    ms = jnp.mean(x * x, axis=-1, keepdims=True)
    return x * lax.rsqrt(ms + EPS) * g


def _sigmoid(x):
    return 1.0 / (1.0 + jnp.exp2(x * (-LOG2E)))


def _resident(shape):
    nd = len(shape)
    return pl.BlockSpec(shape, lambda *_: (0,) * nd, pipeline_mode=pl.Buffered(1))


def _ffn_kernel(x_ref, gpre_ref, wg_ref, wu_ref, wd_ref, gpost_ref, o_ref, h_ref, *, row_split):
    j = pl.program_id(1)
    last = pl.num_programs(1) - 1
    tm, d = x_ref.shape
    n_norm = tm // NORM_ROWS

    @pl.when(j == 0)
    def _():
        def body(i, carry):
            r = pl.ds(pl.multiple_of(i * NORM_ROWS, NORM_ROWS), NORM_ROWS)
            h_ref[r, :] = _rms(x_ref[r, :], gpre_ref[...]).astype(BF16)
            o_ref[r, :] = jnp.zeros((NORM_ROWS, d), F32)
            return carry
        lax.fori_loop(0, n_norm, body, 0, unroll=min(8, n_norm))

    hm = tm // row_split
    for r in range(row_split):
        rows = slice(r * hm, (r + 1) * hm)
        h = h_ref[rows, :]
        g = jnp.dot(h, wg_ref[...], preferred_element_type=F32)
        u = jnp.dot(h, wu_ref[...], preferred_element_type=F32)
        a = (g * _sigmoid(g) * u).astype(BF16)
        o_ref[rows, :] += jnp.dot(a, wd_ref[...], preferred_element_type=F32)

    @pl.when(j == last)
    def _():
        gp = 0.5 * gpost_ref[...]
        for i in range(n_norm):
            r = slice(i * NORM_ROWS, (i + 1) * NORM_ROWS)
            o_ref[r, :] = x_ref[r, :] + _rms(o_ref[r, :], gp)


def _ffn(x, g_pre, wg, wu, wd, g_post, *, tm=1024, tf=512):
    t, d = x.shape
    f = wg.shape[1]
    tm = _tile(t, tm)
    tf = _tile(f, tf)
    row_split = 2 if tm % (2 * NORM_ROWS) == 0 else 1
    return pl.pallas_call(
        functools.partial(_ffn_kernel, row_split=row_split),
        out_shape=jax.ShapeDtypeStruct((t, d), F32),
        grid=(t // tm, f // tf),
        in_specs=[
            pl.BlockSpec((tm, d), lambda i, j: (i, 0)),
            pl.BlockSpec((1, d), lambda i, j: (0, 0)),
            pl.BlockSpec((d, tf), lambda i, j: (0, j)),
            pl.BlockSpec((d, tf), lambda i, j: (0, j)),
            pl.BlockSpec((tf, d), lambda i, j: (j, 0)),
            pl.BlockSpec((1, d), lambda i, j: (0, 0)),
        ],
        out_specs=pl.BlockSpec((tm, d), lambda i, j: (i, 0)),
        scratch_shapes=[pltpu.VMEM((tm, d), BF16)],
        compiler_params=pltpu.CompilerParams(
            dimension_semantics=("parallel", "arbitrary"),
            vmem_limit_bytes=VMEM_LIMIT_BYTES),
        name="ffn",
    )(x, g_pre, wg, wu, wd, g_post)


def _rope(xn, cos, sin_lo, sin_hi):
    return (xn * cos
            + pltpu.roll(xn, HEAD_DIM - ROPE_PAIRS, axis=1) * sin_lo
            + pltpu.roll(xn, ROPE_PAIRS, axis=1) * sin_hi)


def _in_proj_kernel(x_ref, g_ref, wq_ref, wk_ref, wv_ref, wx_ref, wy_ref, gq_ref, gk_ref,
                    cos_ref, slo_ref, shi_ref,
                    q_ref, kt_ref, v_ref, xr_ref, yr_ref):
    h = _rms(x_ref[0], g_ref[...]).astype(BF16)
    cos = cos_ref[...]
    slo = slo_ref[...]
    shi = shi_ref[...]

    q = jnp.dot(h, wq_ref[...], preferred_element_type=F32)
    gq = gq_ref[...] * (HEAD_DIM ** -0.5 * LOG2E)
    for hd in range(N_Q_HEADS):
        sl = slice(hd * HEAD_DIM, (hd + 1) * HEAD_DIM)
        q_ref[0, :, sl] = _rope(_rms(q[:, sl], gq), cos, slo, shi).astype(BF16)

    k = jnp.dot(h, wk_ref[...], preferred_element_type=F32)
    for hd in range(N_KV_HEADS):
        sl = slice(hd * HEAD_DIM, (hd + 1) * HEAD_DIM)
        kr = _rope(_rms(k[:, sl], gk_ref[...]), cos, slo, shi)
        kt_ref[0, hd] = kr.T.astype(BF16)

    v_ref[0] = jnp.dot(h, wv_ref[...], preferred_element_type=F32).astype(BF16)
    xr_ref[0] = jnp.dot(h, wx_ref[...], preferred_element_type=F32)
    yr_ref[0] = jnp.dot(h, wy_ref[...], preferred_element_type=F32)


def _in_proj(x, g, wq, wk, wv, wx, wy, gq, gk, cos, slo, shi, *, tm=512):
    b, s, d = x.shape
    tm = _tile(s, tm)
    tok = lambda w: pl.BlockSpec((1, tm, w), lambda i, j: (i, j, 0))
    tab = pl.BlockSpec((tm, HEAD_DIM), lambda i, j: (j, 0))
    return pl.pallas_call(
        _in_proj_kernel,
        out_shape=(
            jax.ShapeDtypeStruct((b, s, ATTN_WIDTH), BF16),
            jax.ShapeDtypeStruct((b, N_KV_HEADS, HEAD_DIM, s), BF16),
            jax.ShapeDtypeStruct((b, s, KV_WIDTH), BF16),
            jax.ShapeDtypeStruct((b, s, RNN_WIDTH), F32),
            jax.ShapeDtypeStruct((b, s, RNN_WIDTH), F32),
        ),
        grid=(b, s // tm),
        in_specs=[
            tok(d), _resident(g.shape),
            _resident(wq.shape), _resident(wk.shape), _resident(wv.shape),
            _resident(wx.shape), _resident(wy.shape),
            _resident(gq.shape), _resident(gk.shape),
            tab, tab, tab,
        ],
        out_specs=(
            tok(ATTN_WIDTH),
            pl.BlockSpec((1, N_KV_HEADS, HEAD_DIM, tm), lambda i, j: (i, 0, 0, j)),
            tok(KV_WIDTH), tok(RNN_WIDTH), tok(RNN_WIDTH),
        ),
        compiler_params=pltpu.CompilerParams(
            dimension_semantics=("parallel", "parallel"),
            vmem_limit_bytes=VMEM_LIMIT_BYTES),
        name="in_proj",
    )(x, g, wq, wk, wv, wx, wy, gq, gk, cos, slo, shi)


def _attn_kernel(q_ref, kt_ref, v_ref, o_ref, *, tk):
    tq = q_ref.shape[1]
    n_chunks = kt_ref.shape[3] // tk

    heads = [slice(hd * HEAD_DIM, (hd + 1) * HEAD_DIM) for hd in range(Q_PER_KV)]

    def chunk(c, carry):
        start = pl.multiple_of(c * tk, tk)
        kt = kt_ref[0, 0, :, pl.ds(start, tk)]
        v = v_ref[0, pl.ds(start, tk), :]
        out = []
        for sl, (m, l, acc) in zip(heads, carry):
            s = jnp.dot(q_ref[0, :, sl], kt, preferred_element_type=F32)
            m_new = jnp.maximum(m, jnp.max(s, axis=-1, keepdims=True))
            p = jnp.exp2(s - m_new)
            alpha = jnp.exp2(m - m_new)
            l = alpha * l + jnp.sum(p, axis=-1, keepdims=True)
            acc = alpha * acc + jnp.dot(p.astype(BF16), v, preferred_element_type=F32)
            out.append((m_new, l, acc))
        return tuple(out)

    init = tuple((jnp.full((tq, 1), -jnp.inf, F32), jnp.zeros((tq, 1), F32),
                  jnp.zeros((tq, HEAD_DIM), F32)) for _ in heads)
    final = lax.fori_loop(0, n_chunks, chunk, init)
    for sl, (_, l, acc) in zip(heads, final):
        o_ref[0, :, sl] = acc / l


def _attention(q, kt, v, *, tq=1024, tk=2048):
    b, s, _ = q.shape
    tq = _tile(s, tq)
    tk = _tile(s, tk)
    gw = Q_PER_KV * HEAD_DIM
    return pl.pallas_call(
        functools.partial(_attn_kernel, tk=tk),
        out_shape=jax.ShapeDtypeStruct((b, s, ATTN_WIDTH), F32),
        grid=(b, N_KV_HEADS, s // tq),
        in_specs=[
            pl.BlockSpec((1, tq, gw), lambda i, g, j: (i, j, g)),
            pl.BlockSpec((1, 1, HEAD_DIM, s), lambda i, g, j: (i, g, 0, 0),
                         pipeline_mode=pl.Buffered(1)),
            pl.BlockSpec((1, s, HEAD_DIM), lambda i, g, j: (i, 0, g),
                         pipeline_mode=pl.Buffered(1)),
        ],
        out_specs=pl.BlockSpec((1, tq, gw), lambda i, g, j: (i, j, g)),
        compiler_params=pltpu.CompilerParams(
            dimension_semantics=("parallel", "parallel", "arbitrary"),
            vmem_limit_bytes=VMEM_LIMIT_BYTES),
        name="attention",
    )(q, kt, v)


def _conv(x_ref, prev_ref, next_ref, wc_ref, bc_ref, ext_ref, xc_ref):
    t = pl.program_id(1)
    nt = pl.num_programs(1)
    tt = x_ref.shape[1]
    ext_ref[0:SUBLANES] = jnp.where(t > 0, prev_ref[0], 0.0)
    ext_ref[SUBLANES:SUBLANES + tt] = x_ref[0]
    ext_ref[SUBLANES + tt:] = jnp.where(t < nt - 1, next_ref[0], 0.0)
    for n in range(RNN_BLOCKS):
        sl = slice(n * RNN_BLOCK_W, (n + 1) * RNN_BLOCK_W)
        ext = ext_ref[:, sl]
        xc = bc_ref[:, sl]
        for j in range(CONV_W):
            shift = (CONV_PAD_L - j) % ext.shape[0]
            tap = pltpu.roll(ext, shift, axis=0) if shift else ext
            xc = xc + tap[SUBLANES:SUBLANES + tt] * wc_ref[j:j + 1, sl]
        xc_ref[0, :, sl] = xc


def _gates(xc_ref, wg_ref, ba_ref, bx_ref, lam_ref, a_ref, u_ref):
    nl = -lam_ref[...]
    decay = (-RG_C * LOG2E) * (jnp.maximum(nl, 0.0) + jnp.log(1.0 + jnp.exp(-jnp.abs(nl))))
    for n in range(RNN_BLOCKS):
        sl = slice(n * RNN_BLOCK_W, (n + 1) * RNN_BLOCK_W)
        xc = xc_ref[0, :, sl]
        gates = jnp.dot(xc.astype(BF16), wg_ref[n], preferred_element_type=F32)
        r = _sigmoid(gates[:, :RNN_BLOCK_W] + ba_ref[:, sl])
        i = _sigmoid(gates[:, RNN_BLOCK_W:] + bx_ref[:, sl])
        a = jnp.exp2(decay[:, sl] * r)
        a_ref[:, sl] = a
        v = 1.0 - a * a
        u_ref[:, sl] = jnp.where(v > 0.0, v * lax.rsqrt(v), 0.0) * (i * xc)


def _rglru_fwd_kernel(x_ref, prev_ref, next_ref, wc_ref, bc_ref, wg_ref, ba_ref, bx_ref, lam_ref,
                      xc_ref, hf_ref, ext_ref, a_ref, u_ref, carry_ref):
    tt = x_ref.shape[1]

    @pl.when(pl.program_id(1) == 0)
    def _():
        carry_ref[...] = jnp.zeros_like(carry_ref)

    _conv(x_ref, prev_ref, next_ref, wc_ref, bc_ref, ext_ref, xc_ref)
    _gates(xc_ref, wg_ref, ba_ref, bx_ref, lam_ref, a_ref, u_ref)

    def step(i, h):
        h = a_ref[pl.ds(i, 1), :] * h + u_ref[pl.ds(i, 1), :]
        hf_ref[0, pl.ds(i, 1), :] = h
        return h

    carry_ref[...] = lax.fori_loop(0, tt, step, carry_ref[...], unroll=8)


def _rglru_bwd_kernel(xc_ref, wg_ref, ba_ref, bx_ref, lam_ref, hf_ref, y_ref,
                      o_ref, a_ref, u_ref, hb_ref, carry_ref):
    tt = xc_ref.shape[1]

    @pl.when(pl.program_id(1) == 0)
    def _():
        carry_ref[...] = jnp.zeros_like(carry_ref)

    _gates(xc_ref, wg_ref, ba_ref, bx_ref, lam_ref, a_ref, u_ref)

    def step(k, h):
        i = tt - 1 - k
        h = a_ref[pl.ds(i, 1), :] * h + u_ref[pl.ds(i, 1), :]
        hb_ref[pl.ds(i, 1), :] = h
        return h

    carry_ref[...] = lax.fori_loop(0, tt, step, carry_ref[...], unroll=8)

    y = y_ref[0]
    gelu = 0.5 * y * (1.0 + jnp.tanh(math.sqrt(2.0 / math.pi) * (y + 0.044715 * (y * y * y))))
    o_ref[0] = (hf_ref[0] + hb_ref[...]) * gelu


def _rglru(xr, yr, wc, bc, wg_f, ba_f, bx_f, lam_f, wg_b, ba_b, bx_b, lam_b, *, tt=512):
    b, s, r = xr.shape
    tt = _tile(s, tt)
    nt = s // tt
    groups = tt // SUBLANES
    last_group = s // SUBLANES - 1
    gate_params = [_resident(wg_f.shape), _resident(ba_f.shape), _resident(bx_f.shape),
                   _resident(lam_f.shape)]
    chunk = (tt, r)
    cparams = pltpu.CompilerParams(dimension_semantics=("parallel", "arbitrary"),
                                   vmem_limit_bytes=VMEM_LIMIT_BYTES)
    full = jax.ShapeDtypeStruct((b, s, r), F32)

    cur = pl.BlockSpec((1, tt, r), lambda i, j: (i, j, 0))
    prev = pl.BlockSpec((1, SUBLANES, r), lambda i, j: (i, jnp.maximum(j * groups - 1, 0), 0))
    nxt = pl.BlockSpec((1, SUBLANES, r),
                       lambda i, j: (i, jnp.minimum((j + 1) * groups, last_group), 0))
    xc, hf = pl.pallas_call(
        _rglru_fwd_kernel,
        out_shape=(full, full),
        grid=(b, nt),
        in_specs=[cur, prev, nxt, _resident(wc.shape), _resident(bc.shape)] + gate_params,
        out_specs=(cur, cur),
        scratch_shapes=[pltpu.VMEM((tt + 2 * SUBLANES, r), F32), pltpu.VMEM(chunk, F32),
                        pltpu.VMEM(chunk, F32), pltpu.VMEM((1, r), F32)],
        compiler_params=cparams,
        name="rglru_fwd",
    )(xr, xr, xr, wc, bc, wg_f, ba_f, bx_f, lam_f)

    rev = pl.BlockSpec((1, tt, r), lambda i, j: (i, nt - 1 - j, 0))
    return pl.pallas_call(
        _rglru_bwd_kernel,
        out_shape=full,
        grid=(b, nt),
        in_specs=[rev] + gate_params + [rev, rev],
        out_specs=rev,
        scratch_shapes=[pltpu.VMEM(chunk, F32), pltpu.VMEM(chunk, F32), pltpu.VMEM(chunk, F32),
                        pltpu.VMEM((1, r), F32)],
        compiler_params=cparams,
        name="rglru_bwd",
    )(xc, wg_b, ba_b, bx_b, lam_b, hf, yr)


def _out_proj_kernel(x_ref, a_ref, r_ref, ga_ref, gr_ref, wa_ref, wr_ref, gp_ref, o_ref):
    an = _rms(a_ref[...], ga_ref[...]).astype(BF16)
    rn = _rms(r_ref[...], gr_ref[...]).astype(BF16)
    y = (jnp.dot(an, wa_ref[...], preferred_element_type=F32)
         + jnp.dot(rn, wr_ref[...], preferred_element_type=F32))
    o_ref[...] = x_ref[...] + _rms(y, gp_ref[...])


def _out_proj(x, attn, rnn, ga, gr, wa, wr, gp, *, tm=512):
    t, d = x.shape
    tm = _tile(t, tm)
    tok = lambda w: pl.BlockSpec((tm, w), lambda i: (i, 0))
    return pl.pallas_call(
        _out_proj_kernel,
        out_shape=jax.ShapeDtypeStruct((t, d), F32),
        grid=(t // tm,),
        in_specs=[tok(d), tok(ATTN_WIDTH), tok(RNN_WIDTH),
                  _resident(ga.shape), _resident(gr.shape),
                  _resident(wa.shape), _resident(wr.shape), _resident(gp.shape)],
        out_specs=tok(d),
        compiler_params=pltpu.CompilerParams(
            dimension_semantics=("parallel",),
            vmem_limit_bytes=VMEM_LIMIT_BYTES),
        name="out_proj",
    )(x, attn, rnn, ga, gr, wa, wr, gp)


def _rope_tables(s):
    pos = jnp.arange(s, dtype=jnp.int32)
    row = (pos // GRID_W).astype(F32)
    col = (pos % GRID_W).astype(F32)
    inv_freq = ROPE_THETA ** (-jnp.arange(ROPE_PAIRS, dtype=F32) / ROPE_PAIRS)
    ang_r = row[:, None] * inv_freq
    ang_c = col[:, None] * inv_freq
    zero = jnp.zeros_like(ang_r)
    cos = jnp.concatenate([jnp.cos(ang_r)] * 2 + [jnp.cos(ang_c)] * 2, axis=1)
    sin_lo = jnp.concatenate([-jnp.sin(ang_r), zero, -jnp.sin(ang_c), zero], axis=1)
    sin_hi = jnp.concatenate([zero, jnp.sin(ang_r), zero, jnp.sin(ang_c)], axis=1)
    return cos, sin_lo, sin_hi


def _prepare(p):
    row = lambda v: v.reshape(1, -1)
    w_in = p["w_in"].astype(BF16)
    c0, c1, c2, c3 = ATTN_WIDTH, ATTN_WIDTH + KV_WIDTH, ATTN_WIDTH + 2 * KV_WIDTH, \
        ATTN_WIDTH + 2 * KV_WIDTH + RNN_WIDTH
    w_out = p["w_out"].astype(BF16)
    out = dict(
        ffn1=(row(p["g_ffn1_pre"]), p["w_ffn1_gate"].astype(BF16), p["w_ffn1_up"].astype(BF16),
              p["w_ffn1_down"].astype(BF16), row(p["g_ffn1_post"])),
        ffn2=(row(p["g_ffn2_pre"]), p["w_ffn2_gate"].astype(BF16), p["w_ffn2_up"].astype(BF16),
              p["w_ffn2_down"].astype(BF16), row(p["g_ffn2_post"])),
        in_proj=(row(p["g_mix_pre"]), w_in[:, :c0], w_in[:, c0:c1], w_in[:, c1:c2],
                 w_in[:, c2:c3], w_in[:, c3:], row(p["g_q"]), row(p["g_k"])),
        rglru=(p["w_conv"], row(p["b_conv"]),
               jnp.concatenate([p["w_a_fwd"], p["w_x_fwd"]], axis=-1).astype(BF16),
               row(p["b_a_fwd"]), row(p["b_x_fwd"]), row(p["lam_fwd"]),
               jnp.concatenate([p["w_a_bwd"], p["w_x_bwd"]], axis=-1).astype(BF16),
               row(p["b_a_bwd"]), row(p["b_x_bwd"]), row(p["lam_bwd"])),
        out_proj=(row(p["g_attn_out"]), row(p["g_rnn_out"]), w_out[:ATTN_WIDTH], w_out[ATTN_WIDTH:],
                  row(p["g_mix_post"])),
    )
    return out


def _layer(x, w, rope):
    b, s, d = x.shape
    x1 = _ffn(x.reshape(b * s, d), *w["ffn1"])
    q, kt, v, xr, yr = _in_proj(x1.reshape(b, s, d), *w["in_proj"], *rope)
    attn = _attention(q, kt, v)
    rnn = _rglru(xr, yr, *w["rglru"])
    x2 = _out_proj(x1, attn.reshape(b * s, -1), rnn.reshape(b * s, -1), *w["out_proj"])
    x3 = _ffn(x2, *w["ffn2"])
    return x3.reshape(b, s, d)


_PARAM_NAMES = (
    "g_ffn1_pre", "w_ffn1_gate", "w_ffn1_up", "w_ffn1_down", "g_ffn1_post",
    "g_mix_pre", "w_in", "g_q", "g_k", "w_conv", "b_conv",
    "w_a_fwd", "b_a_fwd", "w_x_fwd", "b_x_fwd", "lam_fwd",
    "w_a_bwd", "b_a_bwd", "w_x_bwd", "b_x_bwd", "lam_bwd",
    "g_attn_out", "g_rnn_out", "w_out", "g_mix_post",
    "g_ffn2_pre", "w_ffn2_gate", "w_ffn2_up", "w_ffn2_down", "g_ffn2_post")


def kernel(x_prompt, x_sample, g_ffn1_pre, w_ffn1_gate, w_ffn1_up, w_ffn1_down, g_ffn1_post, g_mix_pre, w_in, g_q, g_k, w_conv, b_conv, w_a_fwd, b_a_fwd, w_x_fwd, b_x_fwd, lam_fwd, w_a_bwd, b_a_bwd, w_x_bwd, b_x_bwd, lam_bwd, g_attn_out, g_rnn_out, w_out, g_mix_post, g_ffn2_pre, w_ffn2_gate, w_ffn2_up, w_ffn2_down, g_ffn2_post):
    stacked = (g_ffn1_pre, w_ffn1_gate, w_ffn1_up, w_ffn1_down, g_ffn1_post, g_mix_pre, w_in, g_q,
               g_k, w_conv, b_conv, w_a_fwd, b_a_fwd, w_x_fwd, b_x_fwd, lam_fwd, w_a_bwd, b_a_bwd,
               w_x_bwd, b_x_bwd, lam_bwd, g_attn_out, g_rnn_out, w_out, g_mix_post, g_ffn2_pre,
               w_ffn2_gate, w_ffn2_up, w_ffn2_down, g_ffn2_post)
    depth = g_ffn1_pre.shape[0]
    y_prompt, y_sample = x_prompt, x_sample
    rope = _rope_tables(max(x_prompt.shape[1], x_sample.shape[1]))
    for layer in range(depth):
        w = _prepare({n: a[layer] for n, a in zip(_PARAM_NAMES, stacked)})
        y_prompt = _layer(y_prompt, w, rope)
        y_sample = _layer(y_sample, w, rope)
    return (y_prompt, y_sample)
```

```python
import functools
import math

import jax
import jax.numpy as jnp
from jax import lax
from jax.experimental import pallas as pl
from jax.experimental.pallas import tpu as pltpu

F32 = jnp.float32
BF16 = jnp.bfloat16

EPS = 1e-6
HEAD_DIM = 128
N_Q_HEADS = 8
N_KV_HEADS = 2
Q_PER_KV = N_Q_HEADS // N_KV_HEADS
ATTN_WIDTH = N_Q_HEADS * HEAD_DIM
KV_WIDTH = N_KV_HEADS * HEAD_DIM
RNN_BLOCKS = 8
RNN_BLOCK_W = 128
RNN_WIDTH = RNN_BLOCKS * RNN_BLOCK_W
GRID_W = 64
ROPE_PAIRS = HEAD_DIM // 4
ROPE_THETA = 10000.0
CONV_W = 4
CONV_PAD_L = 2
RG_C = 8.0
LOG2E = math.log2(math.e)
SUBLANES = 8
NORM_ROWS = 32
SAFE_EXP2_RANGE = 100.0
BOUND_SLACK = 1.0 + 2.0 ** -10

VMEM_LIMIT_BYTES = 58 * 1024 * 1024


def _tile(n, want):
    t = min(n, want)
    assert n % t == 0, (n, want)
    return t


def _rms(x, g):
    ms = jnp.mean(x * x, axis=-1, keepdims=True)
    return x * lax.rsqrt(ms + EPS) * g


def _sigmoid(x):
    return 1.0 / (1.0 + jnp.exp2(x * (-LOG2E)))


def _resident(shape):
    nd = len(shape)
    return pl.BlockSpec(shape, lambda *_: (0,) * nd, pipeline_mode=pl.Buffered(1))


def _ffn_kernel(x_ref, gpre_ref, wg_ref, wu_ref, wd_ref, gpost_ref, o_ref, h_ref, *, row_split):
    j = pl.program_id(1)
    last = pl.num_programs(1) - 1
    tm, d = x_ref.shape
    n_norm = tm // NORM_ROWS

    @pl.when(j == 0)
    def _():
        def body(i, carry):
            r = pl.ds(pl.multiple_of(i * NORM_ROWS, NORM_ROWS), NORM_ROWS)
            h_ref[r, :] = _rms(x_ref[r, :], gpre_ref[...]).astype(BF16)
            o_ref[r, :] = jnp.zeros((NORM_ROWS, d), F32)
            return carry
        lax.fori_loop(0, n_norm, body, 0, unroll=min(8, n_norm))

    hm = tm // row_split
    for r in range(row_split):
        rows = slice(r * hm, (r + 1) * hm)
        h = h_ref[rows, :]
        g = jnp.dot(h, wg_ref[...], preferred_element_type=F32)
        u = jnp.dot(h, wu_ref[...], preferred_element_type=F32)
        a = (g * _sigmoid(g) * u).astype(BF16)
        o_ref[rows, :] += jnp.dot(a, wd_ref[...], preferred_element_type=F32)

    @pl.when(j == last)
    def _():
        gp = 0.5 * gpost_ref[...]
        for i in range(n_norm):
            r = slice(i * NORM_ROWS, (i + 1) * NORM_ROWS)
            o_ref[r, :] = x_ref[r, :] + _rms(o_ref[r, :], gp)


def _ffn(x, g_pre, wg, wu, wd, g_post, *, tm=1024, tf=512):
    t, d = x.shape
    f = wg.shape[1]
    tm = _tile(t, tm)
    tf = _tile(f, tf)
    row_split = 2 if tm % (2 * NORM_ROWS) == 0 else 1
    return pl.pallas_call(
        functools.partial(_ffn_kernel, row_split=row_split),
        out_shape=jax.ShapeDtypeStruct((t, d), F32),
        grid=(t // tm, f // tf),
        in_specs=[
            pl.BlockSpec((tm, d), lambda i, j: (i, 0)),
            pl.BlockSpec((1, d), lambda i, j: (0, 0)),
            pl.BlockSpec((d, tf), lambda i, j: (0, j)),
            pl.BlockSpec((d, tf), lambda i, j: (0, j)),
            pl.BlockSpec((tf, d), lambda i, j: (j, 0)),
            pl.BlockSpec((1, d), lambda i, j: (0, 0)),
        ],
        out_specs=pl.BlockSpec((tm, d), lambda i, j: (i, 0)),
        scratch_shapes=[pltpu.VMEM((tm, d), BF16)],
        compiler_params=pltpu.CompilerParams(
            dimension_semantics=("parallel", "arbitrary"),
            vmem_limit_bytes=VMEM_LIMIT_BYTES),
        name="ffn",
    )(x, g_pre, wg, wu, wd, g_post)


def _rope(xn, cos, sin_lo, sin_hi):
    return (xn * cos
            + pltpu.roll(xn, HEAD_DIM - ROPE_PAIRS, axis=1) * sin_lo
            + pltpu.roll(xn, ROPE_PAIRS, axis=1) * sin_hi)


def _in_proj_kernel(x_ref, g_ref, wq_ref, wk_ref, wv_ref, wx_ref, wy_ref, gq_ref, gk_ref,
                    cos_ref, slo_ref, shi_ref,
                    q_ref, kt_ref, v_ref, xr_ref, yr_ref):
    h = _rms(x_ref[0], g_ref[...]).astype(BF16)
    cos = cos_ref[...]
    slo = slo_ref[...]
    shi = shi_ref[...]

    q = jnp.dot(h, wq_ref[...], preferred_element_type=F32)
    gq = gq_ref[...] * (HEAD_DIM ** -0.5 * LOG2E)
    for hd in range(N_Q_HEADS):
        sl = slice(hd * HEAD_DIM, (hd + 1) * HEAD_DIM)
        q_ref[0, :, sl] = _rope(_rms(q[:, sl], gq), cos, slo, shi).astype(BF16)

    k = jnp.dot(h, wk_ref[...], preferred_element_type=F32)
    for hd in range(N_KV_HEADS):
        sl = slice(hd * HEAD_DIM, (hd + 1) * HEAD_DIM)
        kr = _rope(_rms(k[:, sl], gk_ref[...]), cos, slo, shi)
        kt_ref[0, hd] = kr.T.astype(BF16)

    v_ref[0] = jnp.dot(h, wv_ref[...], preferred_element_type=F32).astype(BF16)
    xr_ref[0] = jnp.dot(h, wx_ref[...], preferred_element_type=F32)
    yr_ref[0] = jnp.dot(h, wy_ref[...], preferred_element_type=F32)


def _in_proj(x, g, wq, wk, wv, wx, wy, gq, gk, cos, slo, shi, *, tm=512):
    b, s, d = x.shape
    tm = _tile(s, tm)
    tok = lambda w: pl.BlockSpec((1, tm, w), lambda i, j: (i, j, 0))
    tab = pl.BlockSpec((tm, HEAD_DIM), lambda i, j: (j, 0))
    return pl.pallas_call(
        _in_proj_kernel,
        out_shape=(
            jax.ShapeDtypeStruct((b, s, ATTN_WIDTH), BF16),
            jax.ShapeDtypeStruct((b, N_KV_HEADS, HEAD_DIM, s), BF16),
            jax.ShapeDtypeStruct((b, s, KV_WIDTH), BF16),
            jax.ShapeDtypeStruct((b, s, RNN_WIDTH), F32),
            jax.ShapeDtypeStruct((b, s, RNN_WIDTH), F32),
        ),
        grid=(b, s // tm),
        in_specs=[
            tok(d), _resident(g.shape),
            _resident(wq.shape), _resident(wk.shape), _resident(wv.shape),
            _resident(wx.shape), _resident(wy.shape),
            _resident(gq.shape), _resident(gk.shape),
            tab, tab, tab,
        ],
        out_specs=(
            tok(ATTN_WIDTH),
            pl.BlockSpec((1, N_KV_HEADS, HEAD_DIM, tm), lambda i, j: (i, 0, 0, j)),
            tok(KV_WIDTH), tok(RNN_WIDTH), tok(RNN_WIDTH),
        ),
        compiler_params=pltpu.CompilerParams(
            dimension_semantics=("parallel", "parallel"),
            vmem_limit_bytes=VMEM_LIMIT_BYTES),
        name="in_proj",
    )(x, g, wq, wk, wv, wx, wy, gq, gk, cos, slo, shi)


def _attn_kernel(q_ref, kt_ref, v_ref, o_ref, kmax_ref, *, tk):
    tq = q_ref.shape[1]
    n_chunks = kt_ref.shape[3] // tk
    heads = [slice(hd * HEAD_DIM, (hd + 1) * HEAD_DIM) for hd in range(Q_PER_KV)]

    @pl.when(pl.program_id(2) == 0)
    def _():
        kf = kt_ref[0, 0].astype(F32)
        k2 = jnp.max(jnp.sum(kf * kf, axis=0, keepdims=True), axis=-1, keepdims=True)
        kmax_ref[...] = jnp.broadcast_to(jnp.sqrt(k2), kmax_ref.shape)

    kmax = kmax_ref[:, :1]
    bounds = []
    for sl in heads:
        qf = q_ref[0, :, sl].astype(F32)
        bounds.append(jnp.sqrt(jnp.sum(qf * qf, axis=-1, keepdims=True)) * kmax * BOUND_SLACK)
    worst = functools.reduce(jnp.maximum, [jnp.max(bd) for bd in bounds])
    bounded = 2.0 * worst <= SAFE_EXP2_RANGE

    def key_chunk(c):
        start = pl.multiple_of(c * tk, tk)
        return kt_ref[0, 0, :, pl.ds(start, tk)], v_ref[0, pl.ds(start, tk), :]

    @pl.when(bounded)
    def _():
        def chunk(c, carry):
            kt, v = key_chunk(c)
            out = []
            for sl, m, (l, acc) in zip(heads, bounds, carry):
                s = jnp.dot(q_ref[0, :, sl], kt, preferred_element_type=F32)
                p = jnp.exp2(s - m)
                l = l + jnp.sum(p, axis=-1, keepdims=True)
                acc = acc + jnp.dot(p.astype(BF16), v, preferred_element_type=F32)
                out.append((l, acc))
            return tuple(out)

        init = tuple((jnp.zeros((tq, 1), F32), jnp.zeros((tq, HEAD_DIM), F32)) for _ in heads)
        final = lax.fori_loop(0, n_chunks, chunk, init)
        for sl, (l, acc) in zip(heads, final):
            o_ref[0, :, sl] = acc / l

    @pl.when(jnp.logical_not(bounded))
    def _():
        def chunk(c, carry):
            kt, v = key_chunk(c)
            out = []
            for sl, (m, l, acc) in zip(heads, carry):
                s = jnp.dot(q_ref[0, :, sl], kt, preferred_element_type=F32)
                m_new = jnp.maximum(m, jnp.max(s, axis=-1, keepdims=True))
                p = jnp.exp2(s - m_new)
                alpha = jnp.exp2(m - m_new)
                l = alpha * l + jnp.sum(p, axis=-1, keepdims=True)
                acc = alpha * acc + jnp.dot(p.astype(BF16), v, preferred_element_type=F32)
                out.append((m_new, l, acc))
            return tuple(out)

        init = tuple((jnp.full((tq, 1), -jnp.inf, F32), jnp.zeros((tq, 1), F32),
                      jnp.zeros((tq, HEAD_DIM), F32)) for _ in heads)
        final = lax.fori_loop(0, n_chunks, chunk, init)
        for sl, (_, l, acc) in zip(heads, final):
            o_ref[0, :, sl] = acc / l


def _attention(q, kt, v, *, tq=1024, tk=2048):
    b, s, _ = q.shape
    tq = _tile(s, tq)
    tk = _tile(s, tk)
    gw = Q_PER_KV * HEAD_DIM
    return pl.pallas_call(
        functools.partial(_attn_kernel, tk=tk),
        out_shape=jax.ShapeDtypeStruct((b, s, ATTN_WIDTH), F32),
        grid=(b, N_KV_HEADS, s // tq),
        in_specs=[
            pl.BlockSpec((1, tq, gw), lambda i, g, j: (i, j, g)),
            pl.BlockSpec((1, 1, HEAD_DIM, s), lambda i, g, j: (i, g, 0, 0),
                         pipeline_mode=pl.Buffered(1)),
            pl.BlockSpec((1, s, HEAD_DIM), lambda i, g, j: (i, 0, g),
                         pipeline_mode=pl.Buffered(1)),
        ],
        out_specs=pl.BlockSpec((1, tq, gw), lambda i, g, j: (i, j, g)),
        scratch_shapes=[pltpu.VMEM((1, HEAD_DIM), F32)],
        compiler_params=pltpu.CompilerParams(
            dimension_semantics=("parallel", "parallel", "arbitrary"),
            vmem_limit_bytes=VMEM_LIMIT_BYTES),
        name="attention",
    )(q, kt, v)


def _conv(x_ref, prev_ref, next_ref, wc_ref, bc_ref, ext_ref, xc_ref):
    t = pl.program_id(1)
    nt = pl.num_programs(1)
    tt = x_ref.shape[1]
    ext_ref[0:SUBLANES] = jnp.where(t > 0, prev_ref[0], 0.0)
    ext_ref[SUBLANES:SUBLANES + tt] = x_ref[0]
    ext_ref[SUBLANES + tt:] = jnp.where(t < nt - 1, next_ref[0], 0.0)
    for n in range(RNN_BLOCKS):
        sl = slice(n * RNN_BLOCK_W, (n + 1) * RNN_BLOCK_W)
        ext = ext_ref[:, sl]
        xc = bc_ref[:, sl]
        for j in range(CONV_W):
            shift = (CONV_PAD_L - j) % ext.shape[0]
            tap = pltpu.roll(ext, shift, axis=0) if shift else ext
            xc = xc + tap[SUBLANES:SUBLANES + tt] * wc_ref[j:j + 1, sl]
        xc_ref[0, :, sl] = xc


def _gates(xc_ref, wg_ref, ba_ref, bx_ref, lam_ref, a_ref, u_ref):
    nl = -lam_ref[...]
    decay = (-RG_C * LOG2E) * (jnp.maximum(nl, 0.0) + jnp.log(1.0 + jnp.exp(-jnp.abs(nl))))
    for n in range(RNN_BLOCKS):
        sl = slice(n * RNN_BLOCK_W, (n + 1) * RNN_BLOCK_W)
        xc = xc_ref[0, :, sl]
        gates = jnp.dot(xc.astype(BF16), wg_ref[n], preferred_element_type=F32)
        r = _sigmoid(gates[:, :RNN_BLOCK_W] + ba_ref[:, sl])
        i = _sigmoid(gates[:, RNN_BLOCK_W:] + bx_ref[:, sl])
        a = jnp.exp2(decay[:, sl] * r)
        a_ref[:, sl] = a
        v = 1.0 - a * a
        u_ref[:, sl] = jnp.where(v > 0.0, v * lax.rsqrt(v), 0.0) * (i * xc)


def _rglru_fwd_kernel(x_ref, prev_ref, next_ref, wc_ref, bc_ref, wg_ref, ba_ref, bx_ref, lam_ref,
                      xc_ref, hf_ref, ext_ref, a_ref, u_ref, carry_ref):
    tt = x_ref.shape[1]

    @pl.when(pl.program_id(1) == 0)
    def _():
        carry_ref[...] = jnp.zeros_like(carry_ref)

    _conv(x_ref, prev_ref, next_ref, wc_ref, bc_ref, ext_ref, xc_ref)
    _gates(xc_ref, wg_ref, ba_ref, bx_ref, lam_ref, a_ref, u_ref)

    def step(i, h):
        h = a_ref[pl.ds(i, 1), :] * h + u_ref[pl.ds(i, 1), :]
        hf_ref[0, pl.ds(i, 1), :] = h
        return h

    carry_ref[...] = lax.fori_loop(0, tt, step, carry_ref[...], unroll=8)


def _rglru_bwd_kernel(xc_ref, wg_ref, ba_ref, bx_ref, lam_ref, hf_ref, y_ref,
                      o_ref, a_ref, u_ref, hb_ref, carry_ref):
    tt = xc_ref.shape[1]

    @pl.when(pl.program_id(1) == 0)
    def _():
        carry_ref[...] = jnp.zeros_like(carry_ref)

    _gates(xc_ref, wg_ref, ba_ref, bx_ref, lam_ref, a_ref, u_ref)

    def step(k, h):
        i = tt - 1 - k
        h = a_ref[pl.ds(i, 1), :] * h + u_ref[pl.ds(i, 1), :]
        hb_ref[pl.ds(i, 1), :] = h
        return h

    carry_ref[...] = lax.fori_loop(0, tt, step, carry_ref[...], unroll=8)

    y = y_ref[0]
    gelu = 0.5 * y * (1.0 + jnp.tanh(math.sqrt(2.0 / math.pi) * (y + 0.044715 * (y * y * y))))
    o_ref[0] = (hf_ref[0] + hb_ref[...]) * gelu


def _rglru(xr, yr, wc, bc, wg_f, ba_f, bx_f, lam_f, wg_b, ba_b, bx_b, lam_b, *, tt=512):
    b, s, r = xr.shape
    tt = _tile(s, tt)
    nt = s // tt
    groups = tt // SUBLANES
    last_group = s // SUBLANES - 1
    gate_params = [_resident(wg_f.shape), _resident(ba_f.shape), _resident(bx_f.shape),
                   _resident(lam_f.shape)]
    chunk = (tt, r)
    cparams = pltpu.CompilerParams(dimension_semantics=("parallel", "arbitrary"),
                                   vmem_limit_bytes=VMEM_LIMIT_BYTES)
    full = jax.ShapeDtypeStruct((b, s, r), F32)

    cur = pl.BlockSpec((1, tt, r), lambda i, j: (i, j, 0))
    prev = pl.BlockSpec((1, SUBLANES, r), lambda i, j: (i, jnp.maximum(j * groups - 1, 0), 0))
    nxt = pl.BlockSpec((1, SUBLANES, r),
                       lambda i, j: (i, jnp.minimum((j + 1) * groups, last_group), 0))
    xc, hf = pl.pallas_call(
        _rglru_fwd_kernel,
        out_shape=(full, full),
        grid=(b, nt),
        in_specs=[cur, prev, nxt, _resident(wc.shape), _resident(bc.shape)] + gate_params,
        out_specs=(cur, cur),
        scratch_shapes=[pltpu.VMEM((tt + 2 * SUBLANES, r), F32), pltpu.VMEM(chunk, F32),
                        pltpu.VMEM(chunk, F32), pltpu.VMEM((1, r), F32)],
        compiler_params=cparams,
        name="rglru_fwd",
    )(xr, xr, xr, wc, bc, wg_f, ba_f, bx_f, lam_f)

    rev = pl.BlockSpec((1, tt, r), lambda i, j: (i, nt - 1 - j, 0))
    return pl.pallas_call(
        _rglru_bwd_kernel,
        out_shape=full,
        grid=(b, nt),
        in_specs=[rev] + gate_params + [rev, rev],
        out_specs=rev,
        scratch_shapes=[pltpu.VMEM(chunk, F32), pltpu.VMEM(chunk, F32), pltpu.VMEM(chunk, F32),
                        pltpu.VMEM((1, r), F32)],
        compiler_params=cparams,
        name="rglru_bwd",
    )(xc, wg_b, ba_b, bx_b, lam_b, hf, yr)


def _out_proj_kernel(x_ref, a_ref, r_ref, ga_ref, gr_ref, wa_ref, wr_ref, gp_ref, o_ref):
    an = _rms(a_ref[...], ga_ref[...]).astype(BF16)
    rn = _rms(r_ref[...], gr_ref[...]).astype(BF16)
    y = (jnp.dot(an, wa_ref[...], preferred_element_type=F32)
         + jnp.dot(rn, wr_ref[...], preferred_element_type=F32))
    o_ref[...] = x_ref[...] + _rms(y, gp_ref[...])


def _out_proj(x, attn, rnn, ga, gr, wa, wr, gp, *, tm=512):
    t, d = x.shape
    tm = _tile(t, tm)
    tok = lambda w: pl.BlockSpec((tm, w), lambda i: (i, 0))
    return pl.pallas_call(
        _out_proj_kernel,
        out_shape=jax.ShapeDtypeStruct((t, d), F32),
        grid=(t // tm,),
        in_specs=[tok(d), tok(ATTN_WIDTH), tok(RNN_WIDTH),
                  _resident(ga.shape), _resident(gr.shape),
                  _resident(wa.shape), _resident(wr.shape), _resident(gp.shape)],
        out_specs=tok(d),
        compiler_params=pltpu.CompilerParams(
            dimension_semantics=("parallel",),
            vmem_limit_bytes=VMEM_LIMIT_BYTES),
        name="out_proj",
    )(x, attn, rnn, ga, gr, wa, wr, gp)


def _rope_tables(s):
    pos = jnp.arange(s, dtype=jnp.int32)
    row = (pos // GRID_W).astype(F32)
    col = (pos % GRID_W).astype(F32)
    inv_freq = ROPE_THETA ** (-jnp.arange(ROPE_PAIRS, dtype=F32) / ROPE_PAIRS)
    ang_r = row[:, None] * inv_freq
    ang_c = col[:, None] * inv_freq
    zero = jnp.zeros_like(ang_r)
    cos = jnp.concatenate([jnp.cos(ang_r)] * 2 + [jnp.cos(ang_c)] * 2, axis=1)
    sin_lo = jnp.concatenate([-jnp.sin(ang_r), zero, -jnp.sin(ang_c), zero], axis=1)
    sin_hi = jnp.concatenate([zero, jnp.sin(ang_r), zero, jnp.sin(ang_c)], axis=1)
    return cos, sin_lo, sin_hi


def _prepare(p):
    row = lambda v: v.reshape(1, -1)
    w_in = p["w_in"].astype(BF16)
    c0, c1, c2, c3 = ATTN_WIDTH, ATTN_WIDTH + KV_WIDTH, ATTN_WIDTH + 2 * KV_WIDTH, \
        ATTN_WIDTH + 2 * KV_WIDTH + RNN_WIDTH
    w_out = p["w_out"].astype(BF16)
    out = dict(
        ffn1=(row(p["g_ffn1_pre"]), p["w_ffn1_gate"].astype(BF16), p["w_ffn1_up"].astype(BF16),
              p["w_ffn1_down"].astype(BF16), row(p["g_ffn1_post"])),
        ffn2=(row(p["g_ffn2_pre"]), p["w_ffn2_gate"].astype(BF16), p["w_ffn2_up"].astype(BF16),
              p["w_ffn2_down"].astype(BF16), row(p["g_ffn2_post"])),
        in_proj=(row(p["g_mix_pre"]), w_in[:, :c0], w_in[:, c0:c1], w_in[:, c1:c2],
                 w_in[:, c2:c3], w_in[:, c3:], row(p["g_q"]), row(p["g_k"])),
        rglru=(p["w_conv"], row(p["b_conv"]),
               jnp.concatenate([p["w_a_fwd"], p["w_x_fwd"]], axis=-1).astype(BF16),
               row(p["b_a_fwd"]), row(p["b_x_fwd"]), row(p["lam_fwd"]),
               jnp.concatenate([p["w_a_bwd"], p["w_x_bwd"]], axis=-1).astype(BF16),
               row(p["b_a_bwd"]), row(p["b_x_bwd"]), row(p["lam_bwd"])),
        out_proj=(row(p["g_attn_out"]), row(p["g_rnn_out"]), w_out[:ATTN_WIDTH], w_out[ATTN_WIDTH:],
                  row(p["g_mix_post"])),
    )
    return out


def _layer(x, w, rope):
    b, s, d = x.shape
    x1 = _ffn(x.reshape(b * s, d), *w["ffn1"])
    q, kt, v, xr, yr = _in_proj(x1.reshape(b, s, d), *w["in_proj"], *rope)
    attn = _attention(q, kt, v)
    rnn = _rglru(xr, yr, *w["rglru"])
    x2 = _out_proj(x1, attn.reshape(b * s, -1), rnn.reshape(b * s, -1), *w["out_proj"])
    x3 = _ffn(x2, *w["ffn2"])
    return x3.reshape(b, s, d)


_PARAM_NAMES = (
    "g_ffn1_pre", "w_ffn1_gate", "w_ffn1_up", "w_ffn1_down", "g_ffn1_post",
    "g_mix_pre", "w_in", "g_q", "g_k", "w_conv", "b_conv",
    "w_a_fwd", "b_a_fwd", "w_x_fwd", "b_x_fwd", "lam_fwd",
    "w_a_bwd", "b_a_bwd", "w_x_bwd", "b_x_bwd", "lam_bwd",
    "g_attn_out", "g_rnn_out", "w_out", "g_mix_post",
    "g_ffn2_pre", "w_ffn2_gate", "w_ffn2_up", "w_ffn2_down", "g_ffn2_post")


def kernel(x_prompt, x_sample, g_ffn1_pre, w_ffn1_gate, w_ffn1_up, w_ffn1_down, g_ffn1_post, g_mix_pre, w_in, g_q, g_k, w_conv, b_conv, w_a_fwd, b_a_fwd, w_x_fwd, b_x_fwd, lam_fwd, w_a_bwd, b_a_bwd, w_x_bwd, b_x_bwd, lam_bwd, g_attn_out, g_rnn_out, w_out, g_mix_post, g_ffn2_pre, w_ffn2_gate, w_ffn2_up, w_ffn2_down, g_ffn2_post):
    stacked = (g_ffn1_pre, w_ffn1_gate, w_ffn1_up, w_ffn1_down, g_ffn1_post, g_mix_pre, w_in, g_q,
               g_k, w_conv, b_conv, w_a_fwd, b_a_fwd, w_x_fwd, b_x_fwd, lam_fwd, w_a_bwd, b_a_bwd,
               w_x_bwd, b_x_bwd, lam_bwd, g_attn_out, g_rnn_out, w_out, g_mix_post, g_ffn2_pre,
               w_ffn2_gate, w_ffn2_up, w_ffn2_down, g_ffn2_post)
    depth = g_ffn1_pre.shape[0]
    y_prompt, y_sample = x_prompt, x_sample
    rope = _rope_tables(max(x_prompt.shape[1], x_sample.shape[1]))
    for layer in range(depth):
        w = _prepare({n: a[layer] for n, a in zip(_PARAM_NAMES, stacked)})
        y_prompt = _layer(y_prompt, w, rope)
        y_sample = _layer(y_sample, w, rope)
    return (y_prompt, y_sample)
```

```python
import functools
import math

import jax
import jax.numpy as jnp
from jax import lax
from jax.experimental import pallas as pl
from jax.experimental.pallas import tpu as pltpu

F32 = jnp.float32
BF16 = jnp.bfloat16

EPS = 1e-6
HEAD_DIM = 128
N_Q_HEADS = 8
N_KV_HEADS = 2
Q_PER_KV = N_Q_HEADS // N_KV_HEADS
ATTN_WIDTH = N_Q_HEADS * HEAD_DIM
KV_WIDTH = N_KV_HEADS * HEAD_DIM
RNN_BLOCKS = 8
RNN_BLOCK_W = 128
RNN_WIDTH = RNN_BLOCKS * RNN_BLOCK_W
GRID_W = 64
ROPE_PAIRS = HEAD_DIM // 4
ROPE_THETA = 10000.0
CONV_W = 4
CONV_PAD_L = 2
RG_C = 8.0
LOG2E = math.log2(math.e)
SUBLANES = 8
NORM_ROWS = 32
SAFE_EXP2_RANGE = 100.0
BOUND_SLACK = 1.0 + 2.0 ** -10

VMEM_LIMIT_BYTES = 58 * 1024 * 1024


def _tile(n, want):
    t = min(n, want)
    assert n % t == 0, (n, want)
    return t


def _rms(x, g):
    ms = jnp.mean(x * x, axis=-1, keepdims=True)
    return x * lax.rsqrt(ms + EPS) * g


def _sigmoid(x):
    return 1.0 / (1.0 + jnp.exp2(x * (-LOG2E)))


def _resident(shape):
    nd = len(shape)
    return pl.BlockSpec(shape, lambda *_: (0,) * nd, pipeline_mode=pl.Buffered(1))


def _ffn_kernel(x_ref, gpre_ref, wg_ref, wu_ref, wd_ref, gpost_ref, o_ref, h_ref, *, row_split):
    j = pl.program_id(1)
    last = pl.num_programs(1) - 1
    tm, _ = x_ref.shape
    hm = tm // row_split

    def step(is_first, is_last):
        gp = 0.5 * gpost_ref[...]
        for r in range(row_split):
            rows = slice(r * hm, (r + 1) * hm)
            chunks = [slice(r * hm + i * NORM_ROWS, r * hm + (i + 1) * NORM_ROWS)
                      for i in range(hm // NORM_ROWS)]
            if is_first:
                for c in chunks:
                    h_ref[c, :] = _rms(x_ref[c, :], gpre_ref[...]).astype(BF16)
            h = h_ref[rows, :]
            g = jnp.dot(h, wg_ref[...], preferred_element_type=F32)
            u = jnp.dot(h, wu_ref[...], preferred_element_type=F32)
            a = (g * _sigmoid(g) * u).astype(BF16)
            down = jnp.dot(a, wd_ref[...], preferred_element_type=F32)
            if is_first:
                o_ref[rows, :] = down
            else:
                o_ref[rows, :] += down
            if is_last:
                for c in chunks:
                    o_ref[c, :] = x_ref[c, :] + _rms(o_ref[c, :], gp)

    pl.when(j == 0)(lambda: step(True, False))
    pl.when(jnp.logical_and(j > 0, j < last))(lambda: step(False, False))
    pl.when(j == last)(lambda: step(False, True))


def _ffn(x, g_pre, wg, wu, wd, g_post, *, tm=1024, tf=512):
    t, d = x.shape
    f = wg.shape[1]
    tm = _tile(t, tm)
    tf = _tile(f, tf)
    assert f // tf >= 2, "first and last d_ff steps must be distinct grid steps"
    row_split = 2 if tm % (2 * NORM_ROWS) == 0 else 1
    return pl.pallas_call(
        functools.partial(_ffn_kernel, row_split=row_split),
        out_shape=jax.ShapeDtypeStruct((t, d), F32),
        grid=(t // tm, f // tf),
        in_specs=[
            pl.BlockSpec((tm, d), lambda i, j: (i, 0)),
            pl.BlockSpec((1, d), lambda i, j: (0, 0)),
            pl.BlockSpec((d, tf), lambda i, j: (0, j)),
            pl.BlockSpec((d, tf), lambda i, j: (0, j)),
            pl.BlockSpec((tf, d), lambda i, j: (j, 0)),
            pl.BlockSpec((1, d), lambda i, j: (0, 0)),
        ],
        out_specs=pl.BlockSpec((tm, d), lambda i, j: (i, 0)),
        scratch_shapes=[pltpu.VMEM((tm, d), BF16)],
        compiler_params=pltpu.CompilerParams(
            dimension_semantics=("parallel", "arbitrary"),
            vmem_limit_bytes=VMEM_LIMIT_BYTES),
        name="ffn",
    )(x, g_pre, wg, wu, wd, g_post)


def _rope(xn, cos, sin_lo, sin_hi):
    return (xn * cos
            + pltpu.roll(xn, HEAD_DIM - ROPE_PAIRS, axis=1) * sin_lo
            + pltpu.roll(xn, ROPE_PAIRS, axis=1) * sin_hi)


def _in_proj_kernel(x_ref, g_ref, wq_ref, wk_ref, wv_ref, wx_ref, wy_ref, gq_ref, gk_ref,
                    cos_ref, slo_ref, shi_ref,
                    q_ref, kt_ref, v_ref, xr_ref, yr_ref):
    h = _rms(x_ref[0], g_ref[...]).astype(BF16)
    cos = cos_ref[...]
    slo = slo_ref[...]
    shi = shi_ref[...]

    q = jnp.dot(h, wq_ref[...], preferred_element_type=F32)
    gq = gq_ref[...] * (HEAD_DIM ** -0.5 * LOG2E)
    for hd in range(N_Q_HEADS):
        sl = slice(hd * HEAD_DIM, (hd + 1) * HEAD_DIM)
        q_ref[0, :, sl] = _rope(_rms(q[:, sl], gq), cos, slo, shi).astype(BF16)

    k = jnp.dot(h, wk_ref[...], preferred_element_type=F32)
    for hd in range(N_KV_HEADS):
        sl = slice(hd * HEAD_DIM, (hd + 1) * HEAD_DIM)
        kr = _rope(_rms(k[:, sl], gk_ref[...]), cos, slo, shi)
        kt_ref[0, hd] = kr.T.astype(BF16)

    v_ref[0] = jnp.dot(h, wv_ref[...], preferred_element_type=F32).astype(BF16)
    xr_ref[0] = jnp.dot(h, wx_ref[...], preferred_element_type=F32)
    yr_ref[0] = jnp.dot(h, wy_ref[...], preferred_element_type=F32)


def _in_proj(x, g, wq, wk, wv, wx, wy, gq, gk, cos, slo, shi, *, tm=512):
    b, s, d = x.shape
    tm = _tile(s, tm)
    tok = lambda w: pl.BlockSpec((1, tm, w), lambda i, j: (i, j, 0))
    tab = pl.BlockSpec((tm, HEAD_DIM), lambda i, j: (j, 0))
    return pl.pallas_call(
        _in_proj_kernel,
        out_shape=(
            jax.ShapeDtypeStruct((b, s, ATTN_WIDTH), BF16),
            jax.ShapeDtypeStruct((b, N_KV_HEADS, HEAD_DIM, s), BF16),
            jax.ShapeDtypeStruct((b, s, KV_WIDTH), BF16),
            jax.ShapeDtypeStruct((b, s, RNN_WIDTH), F32),
            jax.ShapeDtypeStruct((b, s, RNN_WIDTH), F32),
        ),
        grid=(b, s // tm),
        in_specs=[
            tok(d), _resident(g.shape),
            _resident(wq.shape), _resident(wk.shape), _resident(wv.shape),
            _resident(wx.shape), _resident(wy.shape),
            _resident(gq.shape), _resident(gk.shape),
            tab, tab, tab,
        ],
        out_specs=(
            tok(ATTN_WIDTH),
            pl.BlockSpec((1, N_KV_HEADS, HEAD_DIM, tm), lambda i, j: (i, 0, 0, j)),
            tok(KV_WIDTH), tok(RNN_WIDTH), tok(RNN_WIDTH),
        ),
        compiler_params=pltpu.CompilerParams(
            dimension_semantics=("parallel", "parallel"),
            vmem_limit_bytes=VMEM_LIMIT_BYTES),
        name="in_proj",
    )(x, g, wq, wk, wv, wx, wy, gq, gk, cos, slo, shi)


def _attn_kernel(q_ref, kt_ref, v_ref, o_ref, kmax_ref, *, tk):
    tq = q_ref.shape[1]
    n_chunks = kt_ref.shape[3] // tk
    heads = [slice(hd * HEAD_DIM, (hd + 1) * HEAD_DIM) for hd in range(Q_PER_KV)]

    @pl.when(pl.program_id(2) == 0)
    def _():
        kf = kt_ref[0, 0].astype(F32)
        k2 = jnp.max(jnp.sum(kf * kf, axis=0, keepdims=True), axis=-1, keepdims=True)
        kmax_ref[...] = jnp.broadcast_to(jnp.sqrt(k2), kmax_ref.shape)

    kmax = kmax_ref[:, :1]
    bounds = []
    for sl in heads:
        qf = q_ref[0, :, sl].astype(F32)
        bounds.append(jnp.sqrt(jnp.sum(qf * qf, axis=-1, keepdims=True)) * kmax * BOUND_SLACK)
    worst = functools.reduce(jnp.maximum, [jnp.max(bd) for bd in bounds])
    bounded = 2.0 * worst <= SAFE_EXP2_RANGE

    def key_chunk(c):
        start = pl.multiple_of(c * tk, tk)
        return kt_ref[0, 0, :, pl.ds(start, tk)], v_ref[0, pl.ds(start, tk), :]

    @pl.when(bounded)
    def _():
        def chunk(c, carry):
            kt, v = key_chunk(c)
            out = []
            for sl, m, (l, acc) in zip(heads, bounds, carry):
                s = jnp.dot(q_ref[0, :, sl], kt, preferred_element_type=F32)
                p = jnp.exp2(s - m)
                l = l + jnp.sum(p, axis=-1, keepdims=True)
                acc = acc + jnp.dot(p.astype(BF16), v, preferred_element_type=F32)
                out.append((l, acc))
            return tuple(out)

        init = tuple((jnp.zeros((tq, 1), F32), jnp.zeros((tq, HEAD_DIM), F32)) for _ in heads)
        final = lax.fori_loop(0, n_chunks, chunk, init)
        for sl, (l, acc) in zip(heads, final):
            o_ref[0, :, sl] = acc / l

    @pl.when(jnp.logical_not(bounded))
    def _():
        def chunk(c, carry):
            kt, v = key_chunk(c)
            out = []
            for sl, (m, l, acc) in zip(heads, carry):
                s = jnp.dot(q_ref[0, :, sl], kt, preferred_element_type=F32)
                m_new = jnp.maximum(m, jnp.max(s, axis=-1, keepdims=True))
                p = jnp.exp2(s - m_new)
                alpha = jnp.exp2(m - m_new)
                l = alpha * l + jnp.sum(p, axis=-1, keepdims=True)
                acc = alpha * acc + jnp.dot(p.astype(BF16), v, preferred_element_type=F32)
                out.append((m_new, l, acc))
            return tuple(out)

        init = tuple((jnp.full((tq, 1), -jnp.inf, F32), jnp.zeros((tq, 1), F32),
                      jnp.zeros((tq, HEAD_DIM), F32)) for _ in heads)
        final = lax.fori_loop(0, n_chunks, chunk, init)
        for sl, (_, l, acc) in zip(heads, final):
            o_ref[0, :, sl] = acc / l


def _attention(q, kt, v, *, tq=1024, tk=2048):
    b, s, _ = q.shape
    tq = _tile(s, tq)
    tk = _tile(s, tk)
    gw = Q_PER_KV * HEAD_DIM
    return pl.pallas_call(
        functools.partial(_attn_kernel, tk=tk),
        out_shape=jax.ShapeDtypeStruct((b, s, ATTN_WIDTH), F32),
        grid=(b, N_KV_HEADS, s // tq),
        in_specs=[
            pl.BlockSpec((1, tq, gw), lambda i, g, j: (i, j, g)),
            pl.BlockSpec((1, 1, HEAD_DIM, s), lambda i, g, j: (i, g, 0, 0),
                         pipeline_mode=pl.Buffered(1)),
            pl.BlockSpec((1, s, HEAD_DIM), lambda i, g, j: (i, 0, g),
                         pipeline_mode=pl.Buffered(1)),
        ],
        out_specs=pl.BlockSpec((1, tq, gw), lambda i, g, j: (i, j, g)),
        scratch_shapes=[pltpu.VMEM((1, HEAD_DIM), F32)],
        compiler_params=pltpu.CompilerParams(
            dimension_semantics=("parallel", "parallel", "arbitrary"),
            vmem_limit_bytes=VMEM_LIMIT_BYTES),
        name="attention",
    )(q, kt, v)


def _conv(x_ref, prev_ref, next_ref, wc_ref, bc_ref, ext_ref, xc_ref):
    t = pl.program_id(1)
    nt = pl.num_programs(1)
    tt = x_ref.shape[1]
    ext_ref[0:SUBLANES] = jnp.where(t > 0, prev_ref[0], 0.0)
    ext_ref[SUBLANES:SUBLANES + tt] = x_ref[0]
    ext_ref[SUBLANES + tt:] = jnp.where(t < nt - 1, next_ref[0], 0.0)
    for n in range(RNN_BLOCKS):
        sl = slice(n * RNN_BLOCK_W, (n + 1) * RNN_BLOCK_W)
        ext = ext_ref[:, sl]
        xc = bc_ref[:, sl]
        for j in range(CONV_W):
            shift = (CONV_PAD_L - j) % ext.shape[0]
            tap = pltpu.roll(ext, shift, axis=0) if shift else ext
            xc = xc + tap[SUBLANES:SUBLANES + tt] * wc_ref[j:j + 1, sl]
        xc_ref[0, :, sl] = xc


def _gates(xc_ref, wg_ref, ba_ref, bx_ref, lam_ref, a_ref, u_ref):
    nl = -lam_ref[...]
    decay = (-RG_C * LOG2E) * (jnp.maximum(nl, 0.0) + jnp.log(1.0 + jnp.exp(-jnp.abs(nl))))
    for n in range(RNN_BLOCKS):
        sl = slice(n * RNN_BLOCK_W, (n + 1) * RNN_BLOCK_W)
        xc = xc_ref[0, :, sl]
        gates = jnp.dot(xc.astype(BF16), wg_ref[n], preferred_element_type=F32)
        r = _sigmoid(gates[:, :RNN_BLOCK_W] + ba_ref[:, sl])
        i = _sigmoid(gates[:, RNN_BLOCK_W:] + bx_ref[:, sl])
        a = jnp.exp2(decay[:, sl] * r)
        a_ref[:, sl] = a
        v = 1.0 - a * a
        u_ref[:, sl] = jnp.where(v > 0.0, v * lax.rsqrt(v), 0.0) * (i * xc)


def _rglru_fwd_kernel(x_ref, prev_ref, next_ref, wc_ref, bc_ref, wg_ref, ba_ref, bx_ref, lam_ref,
                      xc_ref, hf_ref, ext_ref, a_ref, u_ref, carry_ref):
    tt = x_ref.shape[1]

    @pl.when(pl.program_id(1) == 0)
    def _():
        carry_ref[...] = jnp.zeros_like(carry_ref)

    _conv(x_ref, prev_ref, next_ref, wc_ref, bc_ref, ext_ref, xc_ref)
    _gates(xc_ref, wg_ref, ba_ref, bx_ref, lam_ref, a_ref, u_ref)

    def step(i, h):
        h = a_ref[pl.ds(i, 1), :] * h + u_ref[pl.ds(i, 1), :]
        hf_ref[0, pl.ds(i, 1), :] = h
        return h

    carry_ref[...] = lax.fori_loop(0, tt, step, carry_ref[...], unroll=8)


def _rglru_bwd_kernel(xc_ref, wg_ref, ba_ref, bx_ref, lam_ref, hf_ref, y_ref,
                      o_ref, a_ref, u_ref, hb_ref, carry_ref):
    tt = xc_ref.shape[1]

    @pl.when(pl.program_id(1) == 0)
    def _():
        carry_ref[...] = jnp.zeros_like(carry_ref)

    _gates(xc_ref, wg_ref, ba_ref, bx_ref, lam_ref, a_ref, u_ref)

    def step(k, h):
        i = tt - 1 - k
        h = a_ref[pl.ds(i, 1), :] * h + u_ref[pl.ds(i, 1), :]
        hb_ref[pl.ds(i, 1), :] = h
        return h

    carry_ref[...] = lax.fori_loop(0, tt, step, carry_ref[...], unroll=8)

    y = y_ref[0]
    gelu = 0.5 * y * (1.0 + jnp.tanh(math.sqrt(2.0 / math.pi) * (y + 0.044715 * (y * y * y))))
    o_ref[0] = (hf_ref[0] + hb_ref[...]) * gelu


def _rglru(xr, yr, wc, bc, wg_f, ba_f, bx_f, lam_f, wg_b, ba_b, bx_b, lam_b, *, tt=512):
    b, s, r = xr.shape
    tt = _tile(s, tt)
    nt = s // tt
    groups = tt // SUBLANES
    last_group = s // SUBLANES - 1
    gate_params = [_resident(wg_f.shape), _resident(ba_f.shape), _resident(bx_f.shape),
                   _resident(lam_f.shape)]
    chunk = (tt, r)
    cparams = pltpu.CompilerParams(dimension_semantics=("parallel", "arbitrary"),
                                   vmem_limit_bytes=VMEM_LIMIT_BYTES)
    full = jax.ShapeDtypeStruct((b, s, r), F32)

    cur = pl.BlockSpec((1, tt, r), lambda i, j: (i, j, 0))
    prev = pl.BlockSpec((1, SUBLANES, r), lambda i, j: (i, jnp.maximum(j * groups - 1, 0), 0))
    nxt = pl.BlockSpec((1, SUBLANES, r),
                       lambda i, j: (i, jnp.minimum((j + 1) * groups, last_group), 0))
    xc, hf = pl.pallas_call(
        _rglru_fwd_kernel,
        out_shape=(full, full),
        grid=(b, nt),
        in_specs=[cur, prev, nxt, _resident(wc.shape), _resident(bc.shape)] + gate_params,
        out_specs=(cur, cur),
        scratch_shapes=[pltpu.VMEM((tt + 2 * SUBLANES, r), F32), pltpu.VMEM(chunk, F32),
                        pltpu.VMEM(chunk, F32), pltpu.VMEM((1, r), F32)],
        compiler_params=cparams,
        name="rglru_fwd",
    )(xr, xr, xr, wc, bc, wg_f, ba_f, bx_f, lam_f)

    rev = pl.BlockSpec((1, tt, r), lambda i, j: (i, nt - 1 - j, 0))
    return pl.pallas_call(
        _rglru_bwd_kernel,
        out_shape=full,
        grid=(b, nt),
        in_specs=[rev] + gate_params + [rev, rev],
        out_specs=rev,
        scratch_shapes=[pltpu.VMEM(chunk, F32), pltpu.VMEM(chunk, F32), pltpu.VMEM(chunk, F32),
                        pltpu.VMEM((1, r), F32)],
        compiler_params=cparams,
        name="rglru_bwd",
    )(xc, wg_b, ba_b, bx_b, lam_b, hf, yr)


def _out_proj_kernel(x_ref, a_ref, r_ref, ga_ref, gr_ref, wa_ref, wr_ref, gp_ref, o_ref):
    an = _rms(a_ref[...], ga_ref[...]).astype(BF16)
    rn = _rms(r_ref[...], gr_ref[...]).astype(BF16)
    y = (jnp.dot(an, wa_ref[...], preferred_element_type=F32)
         + jnp.dot(rn, wr_ref[...], preferred_element_type=F32))
    o_ref[...] = x_ref[...] + _rms(y, gp_ref[...])


def _out_proj(x, attn, rnn, ga, gr, wa, wr, gp, *, tm=512):
    t, d = x.shape
    tm = _tile(t, tm)
    tok = lambda w: pl.BlockSpec((tm, w), lambda i: (i, 0))
    return pl.pallas_call(
        _out_proj_kernel,
        out_shape=jax.ShapeDtypeStruct((t, d), F32),
        grid=(t // tm,),
        in_specs=[tok(d), tok(ATTN_WIDTH), tok(RNN_WIDTH),
                  _resident(ga.shape), _resident(gr.shape),
                  _resident(wa.shape), _resident(wr.shape), _resident(gp.shape)],
        out_specs=tok(d),
        compiler_params=pltpu.CompilerParams(
            dimension_semantics=("parallel",),
            vmem_limit_bytes=VMEM_LIMIT_BYTES),
        name="out_proj",
    )(x, attn, rnn, ga, gr, wa, wr, gp)


def _rope_tables(s):
    n_rows = s // GRID_W
    inv_freq = ROPE_THETA ** (-jnp.arange(ROPE_PAIRS, dtype=F32) / ROPE_PAIRS)
    ang_r = jnp.arange(n_rows, dtype=F32)[:, None] * inv_freq
    ang_c = jnp.arange(GRID_W, dtype=F32)[:, None] * inv_freq
    shape = (n_rows, GRID_W, ROPE_PAIRS)
    by_row = lambda a: jnp.broadcast_to(a[:, None, :], shape).reshape(s, ROPE_PAIRS)
    by_col = lambda a: jnp.broadcast_to(a[None, :, :], shape).reshape(s, ROPE_PAIRS)
    cr, sr = by_row(jnp.cos(ang_r)), by_row(jnp.sin(ang_r))
    cc, sc = by_col(jnp.cos(ang_c)), by_col(jnp.sin(ang_c))
    zero = jnp.zeros_like(cr)
    cos = jnp.concatenate([cr, cr, cc, cc], axis=1)
    sin_lo = jnp.concatenate([-sr, zero, -sc, zero], axis=1)
    sin_hi = jnp.concatenate([zero, sr, zero, sc], axis=1)
    return cos, sin_lo, sin_hi


def _prepare(p):
    row = lambda v: v.reshape(1, -1)
    w_in = p["w_in"].astype(BF16)
    c0, c1, c2, c3 = ATTN_WIDTH, ATTN_WIDTH + KV_WIDTH, ATTN_WIDTH + 2 * KV_WIDTH, \
        ATTN_WIDTH + 2 * KV_WIDTH + RNN_WIDTH
    w_out = p["w_out"].astype(BF16)
    out = dict(
        ffn1=(row(p["g_ffn1_pre"]), p["w_ffn1_gate"].astype(BF16), p["w_ffn1_up"].astype(BF16),
              p["w_ffn1_down"].astype(BF16), row(p["g_ffn1_post"])),
        ffn2=(row(p["g_ffn2_pre"]), p["w_ffn2_gate"].astype(BF16), p["w_ffn2_up"].astype(BF16),
              p["w_ffn2_down"].astype(BF16), row(p["g_ffn2_post"])),
        in_proj=(row(p["g_mix_pre"]), w_in[:, :c0], w_in[:, c0:c1], w_in[:, c1:c2],
                 w_in[:, c2:c3], w_in[:, c3:], row(p["g_q"]), row(p["g_k"])),
        rglru=(p["w_conv"], row(p["b_conv"]),
               jnp.concatenate([p["w_a_fwd"], p["w_x_fwd"]], axis=-1).astype(BF16),
               row(p["b_a_fwd"]), row(p["b_x_fwd"]), row(p["lam_fwd"]),
               jnp.concatenate([p["w_a_bwd"], p["w_x_bwd"]], axis=-1).astype(BF16),
               row(p["b_a_bwd"]), row(p["b_x_bwd"]), row(p["lam_bwd"])),
        out_proj=(row(p["g_attn_out"]), row(p["g_rnn_out"]), w_out[:ATTN_WIDTH], w_out[ATTN_WIDTH:],
                  row(p["g_mix_post"])),
    )
    return out


def _layer(x, w, rope):
    b, s, d = x.shape
    x1 = _ffn(x.reshape(b * s, d), *w["ffn1"])
    q, kt, v, xr, yr = _in_proj(x1.reshape(b, s, d), *w["in_proj"], *rope)
    attn = _attention(q, kt, v)
    rnn = _rglru(xr, yr, *w["rglru"])
    x2 = _out_proj(x1, attn.reshape(b * s, -1), rnn.reshape(b * s, -1), *w["out_proj"])
    x3 = _ffn(x2, *w["ffn2"])
    return x3.reshape(b, s, d)


_PARAM_NAMES = (
    "g_ffn1_pre", "w_ffn1_gate", "w_ffn1_up", "w_ffn1_down", "g_ffn1_post",
    "g_mix_pre", "w_in", "g_q", "g_k", "w_conv", "b_conv",
    "w_a_fwd", "b_a_fwd", "w_x_fwd", "b_x_fwd", "lam_fwd",
    "w_a_bwd", "b_a_bwd", "w_x_bwd", "b_x_bwd", "lam_bwd",
    "g_attn_out", "g_rnn_out", "w_out", "g_mix_post",
    "g_ffn2_pre", "w_ffn2_gate", "w_ffn2_up", "w_ffn2_down", "g_ffn2_post")


def kernel(x_prompt, x_sample, g_ffn1_pre, w_ffn1_gate, w_ffn1_up, w_ffn1_down, g_ffn1_post, g_mix_pre, w_in, g_q, g_k, w_conv, b_conv, w_a_fwd, b_a_fwd, w_x_fwd, b_x_fwd, lam_fwd, w_a_bwd, b_a_bwd, w_x_bwd, b_x_bwd, lam_bwd, g_attn_out, g_rnn_out, w_out, g_mix_post, g_ffn2_pre, w_ffn2_gate, w_ffn2_up, w_ffn2_down, g_ffn2_post):
    stacked = (g_ffn1_pre, w_ffn1_gate, w_ffn1_up, w_ffn1_down, g_ffn1_post, g_mix_pre, w_in, g_q,
               g_k, w_conv, b_conv, w_a_fwd, b_a_fwd, w_x_fwd, b_x_fwd, lam_fwd, w_a_bwd, b_a_bwd,
               w_x_bwd, b_x_bwd, lam_bwd, g_attn_out, g_rnn_out, w_out, g_mix_post, g_ffn2_pre,
               w_ffn2_gate, w_ffn2_up, w_ffn2_down, g_ffn2_post)
    depth = g_ffn1_pre.shape[0]
    y_prompt, y_sample = x_prompt, x_sample
    rope = _rope_tables(max(x_prompt.shape[1], x_sample.shape[1]))
    for layer in range(depth):
        w = _prepare({n: a[layer] for n, a in zip(_PARAM_NAMES, stacked)})
        y_prompt = _layer(y_prompt, w, rope)
        y_sample = _layer(y_sample, w, rope)
    return (y_prompt, y_sample)
```

```python
import functools
import math

import jax
import jax.numpy as jnp
from jax import lax
from jax.experimental import pallas as pl
from jax.experimental.pallas import tpu as pltpu

F32 = jnp.float32
BF16 = jnp.bfloat16

EPS = 1e-6
HEAD_DIM = 128
N_Q_HEADS = 8
N_KV_HEADS = 2
Q_PER_KV = N_Q_HEADS // N_KV_HEADS
ATTN_WIDTH = N_Q_HEADS * HEAD_DIM
KV_WIDTH = N_KV_HEADS * HEAD_DIM
RNN_BLOCKS = 8
RNN_BLOCK_W = 128
RNN_WIDTH = RNN_BLOCKS * RNN_BLOCK_W
GRID_W = 64
ROPE_PAIRS = HEAD_DIM // 4
ROPE_THETA = 10000.0
CONV_W = 4
CONV_PAD_L = 2
RG_C = 8.0
LOG2E = math.log2(math.e)
SUBLANES = 8
NORM_ROWS = 32
SAFE_EXP2_RANGE = 100.0
BOUND_SLACK = 1.0 + 2.0 ** -10

VMEM_LIMIT_BYTES = 58 * 1024 * 1024


def _tile(n, want):
    t = min(n, want)
    assert n % t == 0, (n, want)
    return t


def _rms(x, g):
    ms = jnp.mean(x * x, axis=-1, keepdims=True)
    return x * lax.rsqrt(ms + EPS) * g


def _sigmoid(x):
    return 1.0 / (1.0 + jnp.exp2(x * (-LOG2E)))


def _resident(shape):
    nd = len(shape)
    return pl.BlockSpec(shape, lambda *_: (0,) * nd, pipeline_mode=pl.Buffered(1))


def _ffn_kernel(x_ref, gpre_ref, wg_ref, wu_ref, wd_ref, gpost_ref, o_ref, h_ref, *, row_split):
    j = pl.program_id(1)
    last = pl.num_programs(1) - 1
    tm, _ = x_ref.shape
    hm = tm // row_split

    def step(is_first, is_last):
        gp = 0.5 * gpost_ref[...]
        for r in range(row_split):
            rows = slice(r * hm, (r + 1) * hm)
            chunks = [slice(r * hm + i * NORM_ROWS, r * hm + (i + 1) * NORM_ROWS)
                      for i in range(hm // NORM_ROWS)]
            if is_first:
                for c in chunks:
                    h_ref[c, :] = _rms(x_ref[c, :], gpre_ref[...]).astype(BF16)
            h = h_ref[rows, :]
            g = jnp.dot(h, wg_ref[...], preferred_element_type=F32)
            u = jnp.dot(h, wu_ref[...], preferred_element_type=F32)
            a = (g * _sigmoid(g) * u).astype(BF16)
            down = jnp.dot(a, wd_ref[...], preferred_element_type=F32)
            if is_first:
                o_ref[rows, :] = down
            else:
                o_ref[rows, :] += down
            if is_last:
                for c in chunks:
                    o_ref[c, :] = x_ref[c, :] + _rms(o_ref[c, :], gp)

    pl.when(j == 0)(lambda: step(True, False))
    pl.when(jnp.logical_and(j > 0, j < last))(lambda: step(False, False))
    pl.when(j == last)(lambda: step(False, True))


def _ffn(x, g_pre, wg, wu, wd, g_post, *, tm=1024, tf=512):
    t, d = x.shape
    f = wg.shape[1]
    tm = _tile(t, tm)
    tf = _tile(f, tf)
    assert f // tf >= 2, "first and last d_ff steps must be distinct grid steps"
    row_split = 2 if tm % (2 * NORM_ROWS) == 0 else 1
    return pl.pallas_call(
        functools.partial(_ffn_kernel, row_split=row_split),
        out_shape=jax.ShapeDtypeStruct((t, d), F32),
        grid=(t // tm, f // tf),
        in_specs=[
            pl.BlockSpec((tm, d), lambda i, j: (i, 0)),
            pl.BlockSpec((1, d), lambda i, j: (0, 0)),
            pl.BlockSpec((d, tf), lambda i, j: (0, j)),
            pl.BlockSpec((d, tf), lambda i, j: (0, j)),
            pl.BlockSpec((tf, d), lambda i, j: (j, 0)),
            pl.BlockSpec((1, d), lambda i, j: (0, 0)),
        ],
        out_specs=pl.BlockSpec((tm, d), lambda i, j: (i, 0)),
        scratch_shapes=[pltpu.VMEM((tm, d), BF16)],
        compiler_params=pltpu.CompilerParams(
            dimension_semantics=("parallel", "arbitrary"),
            vmem_limit_bytes=VMEM_LIMIT_BYTES),
        name="ffn",
    )(x, g_pre, wg, wu, wd, g_post)


def _rope(xn, cos, sin_lo, sin_hi):
    return (xn * cos
            + pltpu.roll(xn, HEAD_DIM - ROPE_PAIRS, axis=1) * sin_lo
            + pltpu.roll(xn, ROPE_PAIRS, axis=1) * sin_hi)


def _in_proj_kernel(x_ref, g_ref, wq_ref, wk_ref, wv_ref, wx_ref, wy_ref, gq_ref, gk_ref,
                    cos_ref, slo_ref, shi_ref,
                    q_ref, kt_ref, v_ref, xr_ref, yr_ref):
    h = _rms(x_ref[0], g_ref[...]).astype(BF16)
    cos = cos_ref[...]
    slo = slo_ref[...]
    shi = shi_ref[...]

    q = jnp.dot(h, wq_ref[...], preferred_element_type=F32)
    gq = gq_ref[...] * (HEAD_DIM ** -0.5 * LOG2E)
    for hd in range(N_Q_HEADS):
        sl = slice(hd * HEAD_DIM, (hd + 1) * HEAD_DIM)
        q_ref[0, :, sl] = _rope(_rms(q[:, sl], gq), cos, slo, shi).astype(BF16)

    k = jnp.dot(h, wk_ref[...], preferred_element_type=F32)
    for hd in range(N_KV_HEADS):
        sl = slice(hd * HEAD_DIM, (hd + 1) * HEAD_DIM)
        kr = _rope(_rms(k[:, sl], gk_ref[...]), cos, slo, shi)
        kt_ref[0, hd] = kr.T.astype(BF16)

    v_ref[0] = jnp.dot(h, wv_ref[...], preferred_element_type=F32).astype(BF16)
    xr_ref[0] = jnp.dot(h, wx_ref[...], preferred_element_type=F32)
    yr_ref[0] = jnp.dot(h, wy_ref[...], preferred_element_type=F32)


def _in_proj(x, g, wq, wk, wv, wx, wy, gq, gk, cos, slo, shi, *, tm=512):
    b, s, d = x.shape
    tm = _tile(s, tm)
    tok = lambda w: pl.BlockSpec((1, tm, w), lambda i, j: (i, j, 0))
    tab = pl.BlockSpec((tm, HEAD_DIM), lambda i, j: (j, 0))
    return pl.pallas_call(
        _in_proj_kernel,
        out_shape=(
            jax.ShapeDtypeStruct((b, s, ATTN_WIDTH), BF16),
            jax.ShapeDtypeStruct((b, N_KV_HEADS, HEAD_DIM, s), BF16),
            jax.ShapeDtypeStruct((b, s, KV_WIDTH), BF16),
            jax.ShapeDtypeStruct((b, s, RNN_WIDTH), F32),
            jax.ShapeDtypeStruct((b, s, RNN_WIDTH), F32),
        ),
        grid=(b, s // tm),
        in_specs=[
            tok(d), _resident(g.shape),
            _resident(wq.shape), _resident(wk.shape), _resident(wv.shape),
            _resident(wx.shape), _resident(wy.shape),
            _resident(gq.shape), _resident(gk.shape),
            tab, tab, tab,
        ],
        out_specs=(
            tok(ATTN_WIDTH),
            pl.BlockSpec((1, N_KV_HEADS, HEAD_DIM, tm), lambda i, j: (i, 0, 0, j)),
            tok(KV_WIDTH), tok(RNN_WIDTH), tok(RNN_WIDTH),
        ),
        compiler_params=pltpu.CompilerParams(
            dimension_semantics=("parallel", "parallel"),
            vmem_limit_bytes=VMEM_LIMIT_BYTES),
        name="in_proj",
    )(x, g, wq, wk, wv, wx, wy, gq, gk, cos, slo, shi)


def _attn_kernel(q_ref, kt_ref, v_ref, o_ref, kmax_ref, *, tk_bounded, tk_online):
    tq = q_ref.shape[1]
    n_keys = kt_ref.shape[3]
    heads =[slice(hd * HEAD_DIM, (hd + 1) * HEAD_DIM) for hd in range(Q_PER_KV)]

    @pl.when(pl.program_id(2) == 0)
    def _():
        kf = kt_ref[0, 0].astype(F32)
        k2 = jnp.max(jnp.sum(kf * kf, axis=0, keepdims=True), axis=-1, keepdims=True)
        kmax_ref[...] = jnp.broadcast_to(jnp.sqrt(k2), kmax_ref.shape)

    kmax = kmax_ref[:, :1]
    bounds = []
    for sl in heads:
        qf = q_ref[0, :, sl].astype(F32)
        bounds.append(jnp.sqrt(jnp.sum(qf * qf, axis=-1, keepdims=True)) * kmax * BOUND_SLACK)
    worst = functools.reduce(jnp.maximum, [jnp.max(bd) for bd in bounds])
    bounded = 2.0 * worst <= SAFE_EXP2_RANGE

    def key_chunk(c, tk):
        start = pl.multiple_of(c * tk, tk)
        return kt_ref[0, 0, :, pl.ds(start, tk)], v_ref[0, pl.ds(start, tk), :]

    @pl.when(bounded)
    def _():
        def chunk(c, carry):
            kt, v = key_chunk(c, tk_bounded)
            out = []
            for sl, m, (l, acc) in zip(heads, bounds, carry):
                s = jnp.dot(q_ref[0, :, sl], kt, preferred_element_type=F32)
                p = jnp.exp2(s - m)
                l = l + jnp.sum(p, axis=-1, keepdims=True)
                acc = acc + jnp.dot(p.astype(BF16), v, preferred_element_type=F32)
                out.append((l, acc))
            return tuple(out)

        init = tuple((jnp.zeros((tq, 1), F32), jnp.zeros((tq, HEAD_DIM), F32)) for _ in heads)
        final = lax.fori_loop(0, n_keys // tk_bounded, chunk, init)
        for sl, (l, acc) in zip(heads, final):
            o_ref[0, :, sl] = acc / l

    @pl.when(jnp.logical_not(bounded))
    def _():
        def chunk(c, carry):
            kt, v = key_chunk(c, tk_online)
            out = []
            for sl, (m, l, acc) in zip(heads, carry):
                s = jnp.dot(q_ref[0, :, sl], kt, preferred_element_type=F32)
                m_new = jnp.maximum(m, jnp.max(s, axis=-1, keepdims=True))
                p = jnp.exp2(s - m_new)
                alpha = jnp.exp2(m - m_new)
                l = alpha * l + jnp.sum(p, axis=-1, keepdims=True)
                acc = alpha * acc + jnp.dot(p.astype(BF16), v, preferred_element_type=F32)
                out.append((m_new, l, acc))
            return tuple(out)

        init = tuple((jnp.full((tq, 1), -jnp.inf, F32), jnp.zeros((tq, 1), F32),
                      jnp.zeros((tq, HEAD_DIM), F32)) for _ in heads)
        final = lax.fori_loop(0, n_keys // tk_online, chunk, init)
        for sl, (_, l, acc) in zip(heads, final):
            o_ref[0, :, sl] = acc / l


def _attention(q, kt, v, *, tq=1024, tk_bounded=4096, tk_online=2048):
    b, s, _ = q.shape
    tq = _tile(s, tq)
    tk_bounded = _tile(s, tk_bounded)
    tk_online = _tile(s, tk_online)
    gw = Q_PER_KV * HEAD_DIM
    return pl.pallas_call(
        functools.partial(_attn_kernel, tk_bounded=tk_bounded, tk_online=tk_online),
        out_shape=jax.ShapeDtypeStruct((b, s, ATTN_WIDTH), F32),
        grid=(b, N_KV_HEADS, s // tq),
        in_specs=[
            pl.BlockSpec((1, tq, gw), lambda i, g, j: (i, j, g)),
            pl.BlockSpec((1, 1, HEAD_DIM, s), lambda i, g, j: (i, g, 0, 0),
                         pipeline_mode=pl.Buffered(1)),
            pl.BlockSpec((1, s, HEAD_DIM), lambda i, g, j: (i, 0, g),
                         pipeline_mode=pl.Buffered(1)),
        ],
        out_specs=pl.BlockSpec((1, tq, gw), lambda i, g, j: (i, j, g)),
        scratch_shapes=[pltpu.VMEM((1, HEAD_DIM), F32)],
        compiler_params=pltpu.CompilerParams(
            dimension_semantics=("parallel", "parallel", "arbitrary"),
            vmem_limit_bytes=VMEM_LIMIT_BYTES),
        name="attention",
    )(q, kt, v)


def _conv(x_ref, prev_ref, next_ref, wc_ref, bc_ref, ext_ref, xc_ref):
    t = pl.program_id(1)
    nt = pl.num_programs(1)
    tt = x_ref.shape[1]
    ext_ref[0:SUBLANES] = jnp.where(t > 0, prev_ref[0], 0.0)
    ext_ref[SUBLANES:SUBLANES + tt] = x_ref[0]
    ext_ref[SUBLANES + tt:] = jnp.where(t < nt - 1, next_ref[0], 0.0)
    for n in range(RNN_BLOCKS):
        sl = slice(n * RNN_BLOCK_W, (n + 1) * RNN_BLOCK_W)
        ext = ext_ref[:, sl]
        xc = bc_ref[:, sl]
        for j in range(CONV_W):
            shift = (CONV_PAD_L - j) % ext.shape[0]
            tap = pltpu.roll(ext, shift, axis=0) if shift else ext
            xc = xc + tap[SUBLANES:SUBLANES + tt] * wc_ref[j:j + 1, sl]
        xc_ref[0, :, sl] = xc


def _gates(xc_ref, wg_ref, ba_ref, bx_ref, lam_ref, a_ref, u_ref):
    nl = -lam_ref[...]
    decay = (-RG_C * LOG2E) * (jnp.maximum(nl, 0.0) + jnp.log(1.0 + jnp.exp(-jnp.abs(nl))))
    for n in range(RNN_BLOCKS):
        sl = slice(n * RNN_BLOCK_W, (n + 1) * RNN_BLOCK_W)
        xc = xc_ref[0, :, sl]
        gates = jnp.dot(xc.astype(BF16), wg_ref[n], preferred_element_type=F32)
        r = _sigmoid(gates[:, :RNN_BLOCK_W] + ba_ref[:, sl])
        i = _sigmoid(gates[:, RNN_BLOCK_W:] + bx_ref[:, sl])
        a = jnp.exp2(decay[:, sl] * r)
        a_ref[:, sl] = a
        v = 1.0 - a * a
        u_ref[:, sl] = jnp.where(v > 0.0, v * lax.rsqrt(v), 0.0) * (i * xc)


def _rglru_fwd_kernel(x_ref, prev_ref, next_ref, wc_ref, bc_ref, wg_ref, ba_ref, bx_ref, lam_ref,
                      xc_ref, hf_ref, ext_ref, a_ref, u_ref, carry_ref):
    tt = x_ref.shape[1]

    @pl.when(pl.program_id(1) == 0)
    def _():
        carry_ref[...] = jnp.zeros_like(carry_ref)

    _conv(x_ref, prev_ref, next_ref, wc_ref, bc_ref, ext_ref, xc_ref)
    _gates(xc_ref, wg_ref, ba_ref, bx_ref, lam_ref, a_ref, u_ref)

    def step(i, h):
        h = a_ref[pl.ds(i, 1), :] * h + u_ref[pl.ds(i, 1), :]
        hf_ref[0, pl.ds(i, 1), :] = h
        return h

    carry_ref[...] = lax.fori_loop(0, tt, step, carry_ref[...], unroll=8)


def _rglru_bwd_kernel(xc_ref, wg_ref, ba_ref, bx_ref, lam_ref, hf_ref, y_ref,
                      o_ref, a_ref, u_ref, hb_ref, carry_ref):
    tt = xc_ref.shape[1]

    @pl.when(pl.program_id(1) == 0)
    def _():
        carry_ref[...] = jnp.zeros_like(carry_ref)

    _gates(xc_ref, wg_ref, ba_ref, bx_ref, lam_ref, a_ref, u_ref)

    def step(k, h):
        i = tt - 1 - k
        h = a_ref[pl.ds(i, 1), :] * h + u_ref[pl.ds(i, 1), :]
        hb_ref[pl.ds(i, 1), :] = h
        return h

    carry_ref[...] = lax.fori_loop(0, tt, step, carry_ref[...], unroll=8)

    y = y_ref[0]
    gelu = 0.5 * y * (1.0 + jnp.tanh(math.sqrt(2.0 / math.pi) * (y + 0.044715 * (y * y * y))))
    o_ref[0] = (hf_ref[0] + hb_ref[...]) * gelu


def _rglru(xr, yr, wc, bc, wg_f, ba_f, bx_f, lam_f, wg_b, ba_b, bx_b, lam_b, *, tt=512):
    b, s, r = xr.shape
    tt = _tile(s, tt)
    nt = s // tt
    groups = tt // SUBLANES
    last_group = s // SUBLANES - 1
    gate_params = [_resident(wg_f.shape), _resident(ba_f.shape), _resident(bx_f.shape),
                   _resident(lam_f.shape)]
    chunk = (tt, r)
    cparams = pltpu.CompilerParams(dimension_semantics=("parallel", "arbitrary"),
                                   vmem_limit_bytes=VMEM_LIMIT_BYTES)
    full = jax.ShapeDtypeStruct((b, s, r), F32)

    cur = pl.BlockSpec((1, tt, r), lambda i, j: (i, j, 0))
    prev = pl.BlockSpec((1, SUBLANES, r), lambda i, j: (i, jnp.maximum(j * groups - 1, 0), 0))
    nxt = pl.BlockSpec((1, SUBLANES, r),
                       lambda i, j: (i, jnp.minimum((j + 1) * groups, last_group), 0))
    xc, hf = pl.pallas_call(
        _rglru_fwd_kernel,
        out_shape=(full, full),
        grid=(b, nt),
        in_specs=[cur, prev, nxt, _resident(wc.shape), _resident(bc.shape)] + gate_params,
        out_specs=(cur, cur),
        scratch_shapes=[pltpu.VMEM((tt + 2 * SUBLANES, r), F32), pltpu.VMEM(chunk, F32),
                        pltpu.VMEM(chunk, F32), pltpu.VMEM((1, r), F32)],
        compiler_params=cparams,
        name="rglru_fwd",
    )(xr, xr, xr, wc, bc, wg_f, ba_f, bx_f, lam_f)

    rev = pl.BlockSpec((1, tt, r), lambda i, j: (i, nt - 1 - j, 0))
    return pl.pallas_call(
        _rglru_bwd_kernel,
        out_shape=full,
        grid=(b, nt),
        in_specs=[rev] + gate_params + [rev, rev],
        out_specs=rev,
        scratch_shapes=[pltpu.VMEM(chunk, F32), pltpu.VMEM(chunk, F32), pltpu.VMEM(chunk, F32),
                        pltpu.VMEM((1, r), F32)],
        compiler_params=cparams,
        name="rglru_bwd",
    )(xc, wg_b, ba_b, bx_b, lam_b, hf, yr)


def _out_proj_kernel(x_ref, a_ref, r_ref, ga_ref, gr_ref, wa_ref, wr_ref, gp_ref, o_ref):
    an = _rms(a_ref[...], ga_ref[...]).astype(BF16)
    rn = _rms(r_ref[...], gr_ref[...]).astype(BF16)
    y = (jnp.dot(an, wa_ref[...], preferred_element_type=F32)
         + jnp.dot(rn, wr_ref[...], preferred_element_type=F32))
    o_ref[...] = x_ref[...] + _rms(y, gp_ref[...])


def _out_proj(x, attn, rnn, ga, gr, wa, wr, gp, *, tm=512):
    t, d = x.shape
    tm = _tile(t, tm)
    tok = lambda w: pl.BlockSpec((tm, w), lambda i: (i, 0))
    return pl.pallas_call(
        _out_proj_kernel,
        out_shape=jax.ShapeDtypeStruct((t, d), F32),
        grid=(t // tm,),
        in_specs=[tok(d), tok(ATTN_WIDTH), tok(RNN_WIDTH),
                  _resident(ga.shape), _resident(gr.shape),
                  _resident(wa.shape), _resident(wr.shape), _resident(gp.shape)],
        out_specs=tok(d),
        compiler_params=pltpu.CompilerParams(
            dimension_semantics=("parallel",),
            vmem_limit_bytes=VMEM_LIMIT_BYTES),
        name="out_proj",
    )(x, attn, rnn, ga, gr, wa, wr, gp)


def _rope_tables(s):
    n_rows = s // GRID_W
    lane = jnp.arange(HEAD_DIM)
    inv_freq = ROPE_THETA ** (-jnp.arange(ROPE_PAIRS, dtype=F32) / ROPE_PAIRS)
    inv_lane = inv_freq[lane % ROPE_PAIRS]
    uses_col = lane >= 2 * ROPE_PAIRS
    upper = (lane % (2 * ROPE_PAIRS)) >= ROPE_PAIRS
    ang_r = (jnp.arange(n_rows, dtype=F32)[:, None] * inv_lane)[:, None, :]
    ang_c = (jnp.arange(GRID_W, dtype=F32)[:, None] * inv_lane)[None, :, :]
    cos = jnp.where(uses_col, jnp.cos(ang_c), jnp.cos(ang_r)).reshape(s, HEAD_DIM)
    sin = jnp.where(uses_col, jnp.sin(ang_c), jnp.sin(ang_r)).reshape(s, HEAD_DIM)
    sin_lo = jnp.where(upper, 0.0, -sin)
    sin_hi = jnp.where(upper, sin, 0.0)
    return cos, sin_lo, sin_hi


def _prepare(p):
    row = lambda v: v.reshape(1, -1)
    w_in = p["w_in"].astype(BF16)
    c0, c1, c2, c3 = ATTN_WIDTH, ATTN_WIDTH + KV_WIDTH, ATTN_WIDTH + 2 * KV_WIDTH, \
        ATTN_WIDTH + 2 * KV_WIDTH + RNN_WIDTH
    w_out = p["w_out"].astype(BF16)
    out = dict(
        ffn1=(row(p["g_ffn1_pre"]), p["w_ffn1_gate"].astype(BF16), p["w_ffn1_up"].astype(BF16),
              p["w_ffn1_down"].astype(BF16), row(p["g_ffn1_post"])),
        ffn2=(row(p["g_ffn2_pre"]), p["w_ffn2_gate"].astype(BF16), p["w_ffn2_up"].astype(BF16),
              p["w_ffn2_down"].astype(BF16), row(p["g_ffn2_post"])),
        in_proj=(row(p["g_mix_pre"]), w_in[:, :c0], w_in[:, c0:c1], w_in[:, c1:c2],
                 w_in[:, c2:c3], w_in[:, c3:], row(p["g_q"]), row(p["g_k"])),
        rglru=(p["w_conv"], row(p["b_conv"]),
               jnp.concatenate([p["w_a_fwd"], p["w_x_fwd"]], axis=-1).astype(BF16),
               row(p["b_a_fwd"]), row(p["b_x_fwd"]), row(p["lam_fwd"]),
               jnp.concatenate([p["w_a_bwd"], p["w_x_bwd"]], axis=-1).astype(BF16),
               row(p["b_a_bwd"]), row(p["b_x_bwd"]), row(p["lam_bwd"])),
        out_proj=(row(p["g_attn_out"]), row(p["g_rnn_out"]), w_out[:ATTN_WIDTH], w_out[ATTN_WIDTH:],
                  row(p["g_mix_post"])),
    )
    return out


def _layer(x, w, rope):
    b, s, d = x.shape
    x1 = _ffn(x.reshape(b * s, d), *w["ffn1"])
    q, kt, v, xr, yr = _in_proj(x1.reshape(b, s, d), *w["in_proj"], *rope)
    attn = _attention(q, kt, v)
    rnn = _rglru(xr, yr, *w["rglru"])
    x2 = _out_proj(x1, attn.reshape(b * s, -1), rnn.reshape(b * s, -1), *w["out_proj"])
    x3 = _ffn(x2, *w["ffn2"])
    return x3.reshape(b, s, d)


_PARAM_NAMES = (
    "g_ffn1_pre", "w_ffn1_gate", "w_ffn1_up", "w_ffn1_down", "g_ffn1_post",
    "g_mix_pre", "w_in", "g_q", "g_k", "w_conv", "b_conv",
    "w_a_fwd", "b_a_fwd", "w_x_fwd", "b_x_fwd", "lam_fwd",
    "w_a_bwd", "b_a_bwd", "w_x_bwd", "b_x_bwd", "lam_bwd",
    "g_attn_out", "g_rnn_out", "w_out", "g_mix_post",
    "g_ffn2_pre", "w_ffn2_gate", "w_ffn2_up", "w_ffn2_down", "g_ffn2_post")


def kernel(x_prompt, x_sample, g_ffn1_pre, w_ffn1_gate, w_ffn1_up, w_ffn1_down, g_ffn1_post, g_mix_pre, w_in, g_q, g_k, w_conv, b_conv, w_a_fwd, b_a_fwd, w_x_fwd, b_x_fwd, lam_fwd, w_a_bwd, b_a_bwd, w_x_bwd, b_x_bwd, lam_bwd, g_attn_out, g_rnn_out, w_out, g_mix_post, g_ffn2_pre, w_ffn2_gate, w_ffn2_up, w_ffn2_down, g_ffn2_post):
    stacked = (g_ffn1_pre, w_ffn1_gate, w_ffn1_up, w_ffn1_down, g_ffn1_post, g_mix_pre, w_in, g_q,
               g_k, w_conv, b_conv, w_a_fwd, b_a_fwd, w_x_fwd, b_x_fwd, lam_fwd, w_a_bwd, b_a_bwd,
               w_x_bwd, b_x_bwd, lam_bwd, g_attn_out, g_rnn_out, w_out, g_mix_post, g_ffn2_pre,
               w_ffn2_gate, w_ffn2_up, w_ffn2_down, g_ffn2_post)
    depth = g_ffn1_pre.shape[0]
    y_prompt, y_sample = x_prompt, x_sample
    rope = _rope_tables(max(x_prompt.shape[1], x_sample.shape[1]))
    for layer in range(depth):
        w = _prepare({n: a[layer] for n, a in zip(_PARAM_NAMES, stacked)})
        y_prompt = _layer(y_prompt, w, rope)
        y_sample = _layer(y_sample, w, rope)
    return (y_prompt, y_sample)
```

```python
import functools
import math

import jax
import jax.numpy as jnp
from jax import lax
from jax.experimental import pallas as pl
from jax.experimental.pallas import tpu as pltpu

F32 = jnp.float32
BF16 = jnp.bfloat16

EPS = 1e-6
HEAD_DIM = 128
N_Q_HEADS = 8
N_KV_HEADS = 2
Q_PER_KV = N_Q_HEADS // N_KV_HEADS
ATTN_WIDTH = N_Q_HEADS * HEAD_DIM
KV_WIDTH = N_KV_HEADS * HEAD_DIM
RNN_BLOCKS = 8
RNN_BLOCK_W = 128
RNN_WIDTH = RNN_BLOCKS * RNN_BLOCK_W
GRID_W = 64
ROPE_PAIRS = HEAD_DIM // 4
ROPE_THETA = 10000.0
CONV_W = 4
CONV_PAD_L = 2
RG_C = 8.0
LOG2E = math.log2(math.e)
SUBLANES = 8
NORM_ROWS = 32
SAFE_EXP2_RANGE = 100.0
BOUND_SLACK = 1.0 + 2.0 ** -10

VMEM_LIMIT_BYTES = 58 * 1024 * 1024


def _tile(n, want):
    t = min(n, want)
    assert n % t == 0, (n, want)
    return t


def _rms(x, g):
    ms = jnp.mean(x * x, axis=-1, keepdims=True)
    return x * lax.rsqrt(ms + EPS) * g


def _sigmoid(x):
    return 1.0 / (1.0 + jnp.exp2(x * (-LOG2E)))


def _resident(shape):
    nd = len(shape)
    return pl.BlockSpec(shape, lambda *_: (0,) * nd, pipeline_mode=pl.Buffered(1))


def _ffn_kernel(x_ref, gpre_ref, wg_ref, wu_ref, wd_ref, gpost_ref, o_ref, h_ref, *, row_split):
    j = pl.program_id(1)
    last = pl.num_programs(1) - 1
    tm, _ = x_ref.shape
    hm = tm // row_split

    def step(is_first, is_last):
        gp = 0.5 * gpost_ref[...]
        for r in range(row_split):
            rows = slice(r * hm, (r + 1) * hm)
            chunks = [slice(r * hm + i * NORM_ROWS, r * hm + (i + 1) * NORM_ROWS)
                      for i in range(hm // NORM_ROWS)]
            if is_first:
                for c in chunks:
                    h_ref[c, :] = _rms(x_ref[c, :], gpre_ref[...]).astype(BF16)
            h = h_ref[rows, :]
            g = jnp.dot(h, wg_ref[...], preferred_element_type=F32)
            u = jnp.dot(h, wu_ref[...], preferred_element_type=F32)
            a = (g * _sigmoid(g) * u).astype(BF16)
            down = jnp.dot(a, wd_ref[...], preferred_element_type=F32)
            if is_first:
                o_ref[rows, :] = down
            else:
                o_ref[rows, :] += down
            if is_last:
                for c in chunks:
                    o_ref[c, :] = x_ref[c, :] + _rms(o_ref[c, :], gp)

    pl.when(j == 0)(lambda: step(True, False))
    pl.when(jnp.logical_and(j > 0, j < last))(lambda: step(False, False))
    pl.when(j == last)(lambda: step(False, True))


def _ffn(x, g_pre, wg, wu, wd, g_post, *, tm=1024, tf=512):
    t, d = x.shape
    f = wg.shape[1]
    tm = _tile(t, tm)
    tf = _tile(f, tf)
    assert f // tf >= 2, "first and last d_ff steps must be distinct grid steps"
    row_split = 2 if tm % (2 * NORM_ROWS) == 0 else 1
    return pl.pallas_call(
        functools.partial(_ffn_kernel, row_split=row_split),
        out_shape=jax.ShapeDtypeStruct((t, d), F32),
        grid=(t // tm, f // tf),
        in_specs=[
            pl.BlockSpec((tm, d), lambda i, j: (i, 0)),
            pl.BlockSpec((1, d), lambda i, j: (0, 0)),
            pl.BlockSpec((d, tf), lambda i, j: (0, j)),
            pl.BlockSpec((d, tf), lambda i, j: (0, j)),
            pl.BlockSpec((tf, d), lambda i, j: (j, 0)),
            pl.BlockSpec((1, d), lambda i, j: (0, 0)),
        ],
        out_specs=pl.BlockSpec((tm, d), lambda i, j: (i, 0)),
        scratch_shapes=[pltpu.VMEM((tm, d), BF16)],
        compiler_params=pltpu.CompilerParams(
            dimension_semantics=("parallel", "arbitrary"),
            vmem_limit_bytes=VMEM_LIMIT_BYTES),
        name="ffn",
    )(x, g_pre, wg, wu, wd, g_post)


def _rope(xn, cos, sin_lo, sin_hi):
    return (xn * cos
            + pltpu.roll(xn, HEAD_DIM - ROPE_PAIRS, axis=1) * sin_lo
            + pltpu.roll(xn, ROPE_PAIRS, axis=1) * sin_hi)


def _in_proj_kernel(x_ref, g_ref, wq_ref, wk_ref, wv_ref, wx_ref, wy_ref, gq_ref, gk_ref,
                    cos_ref, slo_ref, shi_ref,
                    q_ref, kt_ref, v_ref, xr_ref, yr_ref):
    h = _rms(x_ref[0], g_ref[...]).astype(BF16)
    cos = cos_ref[...]
    slo = slo_ref[...]
    shi = shi_ref[...]

    q = jnp.dot(h, wq_ref[...], preferred_element_type=F32)
    gq = gq_ref[...] * (HEAD_DIM ** -0.5 * LOG2E)
    for hd in range(N_Q_HEADS):
        sl = slice(hd * HEAD_DIM, (hd + 1) * HEAD_DIM)
        q_ref[0, :, sl] = _rope(_rms(q[:, sl], gq), cos, slo, shi).astype(BF16)

    k = jnp.dot(h, wk_ref[...], preferred_element_type=F32)
    for hd in range(N_KV_HEADS):
        sl = slice(hd * HEAD_DIM, (hd + 1) * HEAD_DIM)
        kr = _rope(_rms(k[:, sl], gk_ref[...]), cos, slo, shi)
        kt_ref[0, hd] = kr.T.astype(BF16)

    v_ref[0] = jnp.dot(h, wv_ref[...], preferred_element_type=F32).astype(BF16)
    xr_ref[0] = jnp.dot(h, wx_ref[...], preferred_element_type=F32)
    yr_ref[0] = jnp.dot(h, wy_ref[...], preferred_element_type=F32)


def _in_proj(x, g, wq, wk, wv, wx, wy, gq, gk, cos, slo, shi, *, tm=512):
    b, s, d = x.shape
    tm = _tile(s, tm)
    tok = lambda w: pl.BlockSpec((1, tm, w), lambda i, j: (i, j, 0))
    tab = pl.BlockSpec((tm, HEAD_DIM), lambda i, j: (j, 0))
    return pl.pallas_call(
        _in_proj_kernel,
        out_shape=(
            jax.ShapeDtypeStruct((b, s, ATTN_WIDTH), BF16),
            jax.ShapeDtypeStruct((b, N_KV_HEADS, HEAD_DIM, s), BF16),
            jax.ShapeDtypeStruct((b, s, KV_WIDTH), BF16),
            jax.ShapeDtypeStruct((b, s, RNN_WIDTH), F32),
            jax.ShapeDtypeStruct((b, s, RNN_WIDTH), F32),
        ),
        grid=(b, s // tm),
        in_specs=[
            tok(d), _resident(g.shape),
            _resident(wq.shape), _resident(wk.shape), _resident(wv.shape),
            _resident(wx.shape), _resident(wy.shape),
            _resident(gq.shape), _resident(gk.shape),
            tab, tab, tab,
        ],
        out_specs=(
            tok(ATTN_WIDTH),
            pl.BlockSpec((1, N_KV_HEADS, HEAD_DIM, tm), lambda i, j: (i, 0, 0, j)),
            tok(KV_WIDTH), tok(RNN_WIDTH), tok(RNN_WIDTH),
        ),
        compiler_params=pltpu.CompilerParams(
            dimension_semantics=("parallel", "parallel"),
            vmem_limit_bytes=VMEM_LIMIT_BYTES),
        name="in_proj",
    )(x, g, wq, wk, wv, wx, wy, gq, gk, cos, slo, shi)


def _attn_kernel(q_ref, kt_ref, v_ref, o_ref, kmax_ref, *, tk_bounded, tk_online):
    tq = q_ref.shape[1]
    n_keys = kt_ref.shape[3]
    heads =[slice(hd * HEAD_DIM, (hd + 1) * HEAD_DIM) for hd in range(Q_PER_KV)]

    @pl.when(pl.program_id(2) == 0)
    def _():
        kf = kt_ref[0, 0].astype(F32)
        k2 = jnp.max(jnp.sum(kf * kf, axis=0, keepdims=True), axis=-1, keepdims=True)
        kmax_ref[...] = jnp.broadcast_to(jnp.sqrt(k2), kmax_ref.shape)

    kmax = kmax_ref[:, :1]
    bounds = []
    for sl in heads:
        qf = q_ref[0, :, sl].astype(F32)
        bounds.append(jnp.sqrt(jnp.sum(qf * qf, axis=-1, keepdims=True)) * kmax * BOUND_SLACK)
    worst = functools.reduce(jnp.maximum, [jnp.max(bd) for bd in bounds])
    bounded = 2.0 * worst <= SAFE_EXP2_RANGE

    def key_chunk(c, tk):
        start = pl.multiple_of(c * tk, tk)
        return kt_ref[0, 0, :, pl.ds(start, tk)], v_ref[0, pl.ds(start, tk), :]

    @pl.when(bounded)
    def _():
        def chunk(c, carry):
            kt, v = key_chunk(c, tk_bounded)
            out = []
            for sl, m, (l, acc) in zip(heads, bounds, carry):
                s = jnp.dot(q_ref[0, :, sl], kt, preferred_element_type=F32)
                p = jnp.exp2(s - m)
                l = l + jnp.sum(p, axis=-1, keepdims=True)
                acc = acc + jnp.dot(p.astype(BF16), v, preferred_element_type=F32)
                out.append((l, acc))
            return tuple(out)

        init = tuple((jnp.zeros((tq, 1), F32), jnp.zeros((tq, HEAD_DIM), F32)) for _ in heads)
        final = lax.fori_loop(0, n_keys // tk_bounded, chunk, init)
        for sl, (l, acc) in zip(heads, final):
            o_ref[0, :, sl] = acc / l

    @pl.when(jnp.logical_not(bounded))
    def _():
        def chunk(c, carry):
            kt, v = key_chunk(c, tk_online)
            out = []
            for sl, (m, l, acc) in zip(heads, carry):
                s = jnp.dot(q_ref[0, :, sl], kt, preferred_element_type=F32)
                m_new = jnp.maximum(m, jnp.max(s, axis=-1, keepdims=True))
                p = jnp.exp2(s - m_new)
                alpha = jnp.exp2(m - m_new)
                l = alpha * l + jnp.sum(p, axis=-1, keepdims=True)
                acc = alpha * acc + jnp.dot(p.astype(BF16), v, preferred_element_type=F32)
                out.append((m_new, l, acc))
            return tuple(out)

        init = tuple((jnp.full((tq, 1), -jnp.inf, F32), jnp.zeros((tq, 1), F32),
                      jnp.zeros((tq, HEAD_DIM), F32)) for _ in heads)
        final = lax.fori_loop(0, n_keys // tk_online, chunk, init)
        for sl, (_, l, acc) in zip(heads, final):
            o_ref[0, :, sl] = acc / l


def _attention(q, kt, v, *, tq=1024, tk_bounded=4096, tk_online=2048):
    b, s, _ = q.shape
    tq = _tile(s, tq)
    tk_bounded = _tile(s, tk_bounded)
    tk_online = _tile(s, tk_online)
    gw = Q_PER_KV * HEAD_DIM
    return pl.pallas_call(
        functools.partial(_attn_kernel, tk_bounded=tk_bounded, tk_online=tk_online),
        out_shape=jax.ShapeDtypeStruct((b, s, ATTN_WIDTH), F32),
        grid=(b, N_KV_HEADS, s // tq),
        in_specs=[
            pl.BlockSpec((1, tq, gw), lambda i, g, j: (i, j, g)),
            pl.BlockSpec((1, 1, HEAD_DIM, s), lambda i, g, j: (i, g, 0, 0),
                         pipeline_mode=pl.Buffered(1)),
            pl.BlockSpec((1, s, HEAD_DIM), lambda i, g, j: (i, 0, g),
                         pipeline_mode=pl.Buffered(1)),
        ],
        out_specs=pl.BlockSpec((1, tq, gw), lambda i, g, j: (i, j, g)),
        scratch_shapes=[pltpu.VMEM((1, HEAD_DIM), F32)],
        compiler_params=pltpu.CompilerParams(
            dimension_semantics=("parallel", "parallel", "arbitrary"),
            vmem_limit_bytes=VMEM_LIMIT_BYTES),
        name="attention",
    )(q, kt, v)


def _conv(x_ref, prev_ref, next_ref, wc_ref, bc_ref, ext_ref, xc_ref):
    t = pl.program_id(1)
    nt = pl.num_programs(1)
    tt = x_ref.shape[1]
    ext_ref[0:SUBLANES] = jnp.where(t > 0, prev_ref[0], 0.0)
    ext_ref[SUBLANES:SUBLANES + tt] = x_ref[0]
    ext_ref[SUBLANES + tt:] = jnp.where(t < nt - 1, next_ref[0], 0.0)
    for n in range(RNN_BLOCKS):
        sl = slice(n * RNN_BLOCK_W, (n + 1) * RNN_BLOCK_W)
        ext = ext_ref[:, sl]
        xc = bc_ref[:, sl]
        for j in range(CONV_W):
            shift = (CONV_PAD_L - j) % ext.shape[0]
            tap = pltpu.roll(ext, shift, axis=0) if shift else ext
            xc = xc + tap[SUBLANES:SUBLANES + tt] * wc_ref[j:j + 1, sl]
        xc_ref[0, :, sl] = xc


def _gates(xc_ref, wg_ref, ba_ref, bx_ref, lam_ref, a_ref, u_ref):
    nl = -lam_ref[...]
    half_decay = (-0.5 * RG_C * LOG2E) * (jnp.maximum(nl, 0.0)
                                          + jnp.log(1.0 + jnp.exp(-jnp.abs(nl))))
    half_ba = 0.5 * ba_ref[...]
    half_bx = 0.5 * bx_ref[...]
    for n in range(RNN_BLOCKS):
        sl = slice(n * RNN_BLOCK_W, (n + 1) * RNN_BLOCK_W)
        xc = xc_ref[0, :, sl]
        gates = jnp.dot(xc.astype(BF16), wg_ref[n], preferred_element_type=F32)
        tr = jnp.tanh(gates[:, :RNN_BLOCK_W] + half_ba[:, sl])
        i = 0.5 * jnp.tanh(gates[:, RNN_BLOCK_W:] + half_bx[:, sl]) + 0.5
        a = jnp.exp2(half_decay[:, sl] * tr + half_decay[:, sl])
        a_ref[:, sl] = a
        v = 1.0 - a * a
        u_ref[:, sl] = jnp.where(v > 0.0, v * lax.rsqrt(v), 0.0) * (i * xc)


def _rglru_fwd_kernel(x_ref, prev_ref, next_ref, wc_ref, bc_ref, wg_ref, ba_ref, bx_ref, lam_ref,
                      xc_ref, hf_ref, ext_ref, a_ref, u_ref, carry_ref):
    tt = x_ref.shape[1]

    @pl.when(pl.program_id(1) == 0)
    def _():
        carry_ref[...] = jnp.zeros_like(carry_ref)

    _conv(x_ref, prev_ref, next_ref, wc_ref, bc_ref, ext_ref, xc_ref)
    _gates(xc_ref, wg_ref, ba_ref, bx_ref, lam_ref, a_ref, u_ref)

    def step(i, h):
        h = a_ref[pl.ds(i, 1), :] * h + u_ref[pl.ds(i, 1), :]
        hf_ref[0, pl.ds(i, 1), :] = h
        return h

    carry_ref[...] = lax.fori_loop(0, tt, step, carry_ref[...], unroll=8)


def _rglru_bwd_kernel(xc_ref, wg_ref, ba_ref, bx_ref, lam_ref, hf_ref, y_ref,
                      o_ref, a_ref, u_ref, hb_ref, carry_ref):
    tt = xc_ref.shape[1]

    @pl.when(pl.program_id(1) == 0)
    def _():
        carry_ref[...] = jnp.zeros_like(carry_ref)

    _gates(xc_ref, wg_ref, ba_ref, bx_ref, lam_ref, a_ref, u_ref)

    def step(k, h):
        i = tt - 1 - k
        h = a_ref[pl.ds(i, 1), :] * h + u_ref[pl.ds(i, 1), :]
        hb_ref[pl.ds(i, 1), :] = h
        return h

    carry_ref[...] = lax.fori_loop(0, tt, step, carry_ref[...], unroll=8)

    y = y_ref[0]
    gelu = 0.5 * y * (1.0 + jnp.tanh(math.sqrt(2.0 / math.pi) * (y + 0.044715 * (y * y * y))))
    o_ref[0] = (hf_ref[0] + hb_ref[...]) * gelu


def _rglru(xr, yr, wc, bc, wg_f, ba_f, bx_f, lam_f, wg_b, ba_b, bx_b, lam_b, *, tt=1024):
    b, s, r = xr.shape
    tt = _tile(s, tt)
    nt = s // tt
    groups = tt // SUBLANES
    last_group = s // SUBLANES - 1
    gate_params = [_resident(wg_f.shape), _resident(ba_f.shape), _resident(bx_f.shape),
                   _resident(lam_f.shape)]
    chunk = (tt, r)
    cparams = pltpu.CompilerParams(dimension_semantics=("parallel", "arbitrary"),
                                   vmem_limit_bytes=VMEM_LIMIT_BYTES)
    full = jax.ShapeDtypeStruct((b, s, r), F32)

    cur = pl.BlockSpec((1, tt, r), lambda i, j: (i, j, 0))
    prev = pl.BlockSpec((1, SUBLANES, r), lambda i, j: (i, jnp.maximum(j * groups - 1, 0), 0))
    nxt = pl.BlockSpec((1, SUBLANES, r),
                       lambda i, j: (i, jnp.minimum((j + 1) * groups, last_group), 0))
    xc, hf = pl.pallas_call(
        _rglru_fwd_kernel,
        out_shape=(full, full),
        grid=(b, nt),
        in_specs=[cur, prev, nxt, _resident(wc.shape), _resident(bc.shape)] + gate_params,
        out_specs=(cur, cur),
        scratch_shapes=[pltpu.VMEM((tt + 2 * SUBLANES, r), F32), pltpu.VMEM(chunk, F32),
                        pltpu.VMEM(chunk, F32), pltpu.VMEM((1, r), F32)],
        compiler_params=cparams,
        name="rglru_fwd",
    )(xr, xr, xr, wc, bc, wg_f, ba_f, bx_f, lam_f)

    rev = pl.BlockSpec((1, tt, r), lambda i, j: (i, nt - 1 - j, 0))
    return pl.pallas_call(
        _rglru_bwd_kernel,
        out_shape=full,
        grid=(b, nt),
        in_specs=[rev] + gate_params + [rev, rev],
        out_specs=rev,
        scratch_shapes=[pltpu.VMEM(chunk, F32), pltpu.VMEM(chunk, F32), pltpu.VMEM(chunk, F32),
                        pltpu.VMEM((1, r), F32)],
        compiler_params=cparams,
        name="rglru_bwd",
    )(xc, wg_b, ba_b, bx_b, lam_b, hf, yr)


def _out_proj_kernel(x_ref, a_ref, r_ref, ga_ref, gr_ref, wa_ref, wr_ref, gp_ref, o_ref):
    an = _rms(a_ref[...], ga_ref[...]).astype(BF16)
    rn = _rms(r_ref[...], gr_ref[...]).astype(BF16)
    y = (jnp.dot(an, wa_ref[...], preferred_element_type=F32)
         + jnp.dot(rn, wr_ref[...], preferred_element_type=F32))
    o_ref[...] = x_ref[...] + _rms(y, gp_ref[...])


def _out_proj(x, attn, rnn, ga, gr, wa, wr, gp, *, tm=512):
    t, d = x.shape
    tm = _tile(t, tm)
    tok = lambda w: pl.BlockSpec((tm, w), lambda i: (i, 0))
    return pl.pallas_call(
        _out_proj_kernel,
        out_shape=jax.ShapeDtypeStruct((t, d), F32),
        grid=(t // tm,),
        in_specs=[tok(d), tok(ATTN_WIDTH), tok(RNN_WIDTH),
                  _resident(ga.shape), _resident(gr.shape),
                  _resident(wa.shape), _resident(wr.shape), _resident(gp.shape)],
        out_specs=tok(d),
        compiler_params=pltpu.CompilerParams(
            dimension_semantics=("parallel",),
            vmem_limit_bytes=VMEM_LIMIT_BYTES),
        name="out_proj",
    )(x, attn, rnn, ga, gr, wa, wr, gp)


def _rope_tables(s):
    n_rows = s // GRID_W
    lane = jnp.arange(HEAD_DIM)
    inv_freq = ROPE_THETA ** (-jnp.arange(ROPE_PAIRS, dtype=F32) / ROPE_PAIRS)
    inv_lane = inv_freq[lane % ROPE_PAIRS]
    uses_col = lane >= 2 * ROPE_PAIRS
    upper = (lane % (2 * ROPE_PAIRS)) >= ROPE_PAIRS
    ang_r = (jnp.arange(n_rows, dtype=F32)[:, None] * inv_lane)[:, None, :]
    ang_c = (jnp.arange(GRID_W, dtype=F32)[:, None] * inv_lane)[None, :, :]
    cos = jnp.where(uses_col, jnp.cos(ang_c), jnp.cos(ang_r)).reshape(s, HEAD_DIM)
    sin = jnp.where(uses_col, jnp.sin(ang_c), jnp.sin(ang_r)).reshape(s, HEAD_DIM)
    sin_lo = jnp.where(upper, 0.0, -sin)
    sin_hi = jnp.where(upper, sin, 0.0)
    return cos, sin_lo, sin_hi


def _prepare(p):
    row = lambda v: v.reshape(1, -1)
    w_in = p["w_in"].astype(BF16)
    c0, c1, c2, c3 = ATTN_WIDTH, ATTN_WIDTH + KV_WIDTH, ATTN_WIDTH + 2 * KV_WIDTH, \
        ATTN_WIDTH + 2 * KV_WIDTH + RNN_WIDTH
    w_out = p["w_out"].astype(BF16)
    out = dict(
        ffn1=(row(p["g_ffn1_pre"]), p["w_ffn1_gate"].astype(BF16), p["w_ffn1_up"].astype(BF16),
              p["w_ffn1_down"].astype(BF16), row(p["g_ffn1_post"])),
        ffn2=(row(p["g_ffn2_pre"]), p["w_ffn2_gate"].astype(BF16), p["w_ffn2_up"].astype(BF16),
              p["w_ffn2_down"].astype(BF16), row(p["g_ffn2_post"])),
        in_proj=(row(p["g_mix_pre"]), w_in[:, :c0], w_in[:, c0:c1], w_in[:, c1:c2],
                 w_in[:, c2:c3], w_in[:, c3:], row(p["g_q"]), row(p["g_k"])),
        rglru=(p["w_conv"], row(p["b_conv"]),
               (0.5 * jnp.concatenate([p["w_a_fwd"], p["w_x_fwd"]], axis=-1)).astype(BF16),
               row(p["b_a_fwd"]), row(p["b_x_fwd"]), row(p["lam_fwd"]),
               (0.5 * jnp.concatenate([p["w_a_bwd"], p["w_x_bwd"]], axis=-1)).astype(BF16),
               row(p["b_a_bwd"]), row(p["b_x_bwd"]), row(p["lam_bwd"])),
        out_proj=(row(p["g_attn_out"]), row(p["g_rnn_out"]), w_out[:ATTN_WIDTH], w_out[ATTN_WIDTH:],
                  row(p["g_mix_post"])),
    )
    return out


def _layer(x, w, rope):
    b, s, d = x.shape
    x1 = _ffn(x.reshape(b * s, d), *w["ffn1"])
    q, kt, v, xr, yr = _in_proj(x1.reshape(b, s, d), *w["in_proj"], *rope)
    attn = _attention(q, kt, v)
    rnn = _rglru(xr, yr, *w["rglru"])
    x2 = _out_proj(x1, attn.reshape(b * s, -1), rnn.reshape(b * s, -1), *w["out_proj"])
    x3 = _ffn(x2, *w["ffn2"])
    return x3.reshape(b, s, d)


_PARAM_NAMES = (
    "g_ffn1_pre", "w_ffn1_gate", "w_ffn1_up", "w_ffn1_down", "g_ffn1_post",
    "g_mix_pre", "w_in", "g_q", "g_k", "w_conv", "b_conv",
    "w_a_fwd", "b_a_fwd", "w_x_fwd", "b_x_fwd", "lam_fwd",
    "w_a_bwd", "b_a_bwd", "w_x_bwd", "b_x_bwd", "lam_bwd",
    "g_attn_out", "g_rnn_out", "w_out", "g_mix_post",
    "g_ffn2_pre", "w_ffn2_gate", "w_ffn2_up", "w_ffn2_down", "g_ffn2_post")


def kernel(x_prompt, x_sample, g_ffn1_pre, w_ffn1_gate, w_ffn1_up, w_ffn1_down, g_ffn1_post, g_mix_pre, w_in, g_q, g_k, w_conv, b_conv, w_a_fwd, b_a_fwd, w_x_fwd, b_x_fwd, lam_fwd, w_a_bwd, b_a_bwd, w_x_bwd, b_x_bwd, lam_bwd, g_attn_out, g_rnn_out, w_out, g_mix_post, g_ffn2_pre, w_ffn2_gate, w_ffn2_up, w_ffn2_down, g_ffn2_post):
    stacked = (g_ffn1_pre, w_ffn1_gate, w_ffn1_up, w_ffn1_down, g_ffn1_post, g_mix_pre, w_in, g_q,
               g_k, w_conv, b_conv, w_a_fwd, b_a_fwd, w_x_fwd, b_x_fwd, lam_fwd, w_a_bwd, b_a_bwd,
               w_x_bwd, b_x_bwd, lam_bwd, g_attn_out, g_rnn_out, w_out, g_mix_post, g_ffn2_pre,
               w_ffn2_gate, w_ffn2_up, w_ffn2_down, g_ffn2_post)
    depth = g_ffn1_pre.shape[0]
    y_prompt, y_sample = x_prompt, x_sample
    rope = _rope_tables(max(x_prompt.shape[1], x_sample.shape[1]))
    for layer in range(depth):
        w = _prepare({n: a[layer] for n, a in zip(_PARAM_NAMES, stacked)})
        y_prompt = _layer(y_prompt, w, rope)
        y_sample = _layer(y_sample, w, rope)
    return (y_prompt, y_sample)
```

```python
import functools
import math

import jax
import jax.numpy as jnp
from jax import lax
from jax.experimental import pallas as pl
from jax.experimental.pallas import tpu as pltpu

F32 = jnp.float32
BF16 = jnp.bfloat16

EPS = 1e-6
HEAD_DIM = 128
N_Q_HEADS = 8
N_KV_HEADS = 2
Q_PER_KV = N_Q_HEADS // N_KV_HEADS
ATTN_WIDTH = N_Q_HEADS * HEAD_DIM
KV_WIDTH = N_KV_HEADS * HEAD_DIM
RNN_BLOCKS = 8
RNN_BLOCK_W = 128
RNN_WIDTH = RNN_BLOCKS * RNN_BLOCK_W
GRID_W = 64
ROPE_PAIRS = HEAD_DIM // 4
ROPE_THETA = 10000.0
CONV_W = 4
CONV_PAD_L = 2
RG_C = 8.0
LOG2E = math.log2(math.e)
SUBLANES = 8
NORM_ROWS = 32
SAFE_EXP2_RANGE = 100.0
BOUND_SLACK = 1.0 + 2.0 ** -10

VMEM_LIMIT_BYTES = 58 * 1024 * 1024


def _tile(n, want):
    t = min(n, want)
    assert n % t == 0, (n, want)
    return t


def _rms(x, g):
    ms = jnp.mean(x * x, axis=-1, keepdims=True)
    return x * lax.rsqrt(ms + EPS) * g


def _silu(x):
    h = 0.5 * x
    return h * jnp.tanh(h) + h


def _resident(shape):
    nd = len(shape)
    return pl.BlockSpec(shape, lambda *_: (0,) * nd, pipeline_mode=pl.Buffered(1))


def _ffn_kernel(x_ref, gpre_ref, wg_ref, wu_ref, wd_ref, gpost_ref, o_ref, h_ref, *, row_split):
    j = pl.program_id(1)
    last = pl.num_programs(1) - 1
    tm, _ = x_ref.shape
    hm = tm // row_split

    def step(is_first, is_last):
        gp = 0.5 * gpost_ref[...]
        for r in range(row_split):
            rows = slice(r * hm, (r + 1) * hm)
            chunks = [slice(r * hm + i * NORM_ROWS, r * hm + (i + 1) * NORM_ROWS)
                      for i in range(hm // NORM_ROWS)]
            if is_first:
                for c in chunks:
                    h_ref[c, :] = _rms(x_ref[c, :], gpre_ref[...]).astype(BF16)
            h = h_ref[rows, :]
            g = jnp.dot(h, wg_ref[...], preferred_element_type=F32)
            u = jnp.dot(h, wu_ref[...], preferred_element_type=F32)
            a = (_silu(g) * u).astype(BF16)
            down = jnp.dot(a, wd_ref[...], preferred_element_type=F32)
            if is_first:
                o_ref[rows, :] = down
            else:
                o_ref[rows, :] += down
            if is_last:
                for c in chunks:
                    o_ref[c, :] = x_ref[c, :] + _rms(o_ref[c, :], gp)

    pl.when(j == 0)(lambda: step(True, False))
    pl.when(jnp.logical_and(j > 0, j < last))(lambda: step(False, False))
    pl.when(j == last)(lambda: step(False, True))


def _ffn(x, g_pre, wg, wu, wd, g_post, *, tm=1024, tf=512):
    t, d = x.shape
    f = wg.shape[1]
    tm = _tile(t, tm)
    tf = _tile(f, tf)
    assert f // tf >= 2, "first and last d_ff steps must be distinct grid steps"
    row_split = 2 if tm % (2 * NORM_ROWS) == 0 else 1
    return pl.pallas_call(
        functools.partial(_ffn_kernel, row_split=row_split),
        out_shape=jax.ShapeDtypeStruct((t, d), F32),
        grid=(t // tm, f // tf),
        in_specs=[
            pl.BlockSpec((tm, d), lambda i, j: (i, 0)),
            pl.BlockSpec((1, d), lambda i, j: (0, 0)),
            pl.BlockSpec((d, tf), lambda i, j: (0, j)),
            pl.BlockSpec((d, tf), lambda i, j: (0, j)),
            pl.BlockSpec((tf, d), lambda i, j: (j, 0)),
            pl.BlockSpec((1, d), lambda i, j: (0, 0)),
        ],
        out_specs=pl.BlockSpec((tm, d), lambda i, j: (i, 0)),
        scratch_shapes=[pltpu.VMEM((tm, d), BF16)],
        compiler_params=pltpu.CompilerParams(
            dimension_semantics=("parallel", "arbitrary"),
            vmem_limit_bytes=VMEM_LIMIT_BYTES),
        name="ffn",
    )(x, g_pre, wg, wu, wd, g_post)


def _rope(xn, cos, sin_lo, sin_hi):
    return (xn * cos
            + pltpu.roll(xn, HEAD_DIM - ROPE_PAIRS, axis=1) * sin_lo
            + pltpu.roll(xn, ROPE_PAIRS, axis=1) * sin_hi)


def _in_proj_kernel(x_ref, g_ref, wq_ref, wk_ref, wv_ref, wx_ref, wy_ref, gq_ref, gk_ref,
                    cos_ref, slo_ref, shi_ref,
                    q_ref, kt_ref, v_ref, xr_ref, yr_ref):
    h = _rms(x_ref[0], g_ref[...]).astype(BF16)
    cos = cos_ref[...]
    slo = slo_ref[...]
    shi = shi_ref[...]

    q = jnp.dot(h, wq_ref[...], preferred_element_type=F32)
    gq = gq_ref[...] * (HEAD_DIM ** -0.5 * LOG2E)
    for hd in range(N_Q_HEADS):
        sl = slice(hd * HEAD_DIM, (hd + 1) * HEAD_DIM)
        q_ref[0, :, sl] = _rope(_rms(q[:, sl], gq), cos, slo, shi).astype(BF16)

    k = jnp.dot(h, wk_ref[...], preferred_element_type=F32)
    for hd in range(N_KV_HEADS):
        sl = slice(hd * HEAD_DIM, (hd + 1) * HEAD_DIM)
        kr = _rope(_rms(k[:, sl], gk_ref[...]), cos, slo, shi)
        kt_ref[0, hd] = kr.T.astype(BF16)

    v_ref[0] = jnp.dot(h, wv_ref[...], preferred_element_type=F32).astype(BF16)
    xr_ref[0] = jnp.dot(h, wx_ref[...], preferred_element_type=F32)
    yr_ref[0] = jnp.dot(h, wy_ref[...], preferred_element_type=F32)


def _in_proj(x, g, wq, wk, wv, wx, wy, gq, gk, cos, slo, shi, *, tm=512):
    b, s, d = x.shape
    tm = _tile(s, tm)
    tok = lambda w: pl.BlockSpec((1, tm, w), lambda i, j: (i, j, 0))
    tab = pl.BlockSpec((tm, HEAD_DIM), lambda i, j: (j, 0))
    return pl.pallas_call(
        _in_proj_kernel,
        out_shape=(
            jax.ShapeDtypeStruct((b, s, ATTN_WIDTH), BF16),
            jax.ShapeDtypeStruct((b, N_KV_HEADS, HEAD_DIM, s), BF16),
            jax.ShapeDtypeStruct((b, s, KV_WIDTH), BF16),
            jax.ShapeDtypeStruct((b, s, RNN_WIDTH), F32),
            jax.ShapeDtypeStruct((b, s, RNN_WIDTH), F32),
        ),
        grid=(b, s // tm),
        in_specs=[
            tok(d), _resident(g.shape),
            _resident(wq.shape), _resident(wk.shape), _resident(wv.shape),
            _resident(wx.shape), _resident(wy.shape),
            _resident(gq.shape), _resident(gk.shape),
            tab, tab, tab,
        ],
        out_specs=(
            tok(ATTN_WIDTH),
            pl.BlockSpec((1, N_KV_HEADS, HEAD_DIM, tm), lambda i, j: (i, 0, 0, j)),
            tok(KV_WIDTH), tok(RNN_WIDTH), tok(RNN_WIDTH),
        ),
        compiler_params=pltpu.CompilerParams(
            dimension_semantics=("parallel", "parallel"),
            vmem_limit_bytes=VMEM_LIMIT_BYTES),
        name="in_proj",
    )(x, g, wq, wk, wv, wx, wy, gq, gk, cos, slo, shi)


def _attn_kernel(q_ref, kt_ref, v_ref, o_ref, kmax_ref, *, tk_bounded, tk_online):
    tq = q_ref.shape[1]
    n_keys = kt_ref.shape[3]
    heads =[slice(hd * HEAD_DIM, (hd + 1) * HEAD_DIM) for hd in range(Q_PER_KV)]

    @pl.when(pl.program_id(2) == 0)
    def _():
        kf = kt_ref[0, 0].astype(F32)
        k2 = jnp.max(jnp.sum(kf * kf, axis=0, keepdims=True), axis=-1, keepdims=True)
        kmax_ref[...] = jnp.broadcast_to(jnp.sqrt(k2), kmax_ref.shape)

    kmax = kmax_ref[:, :1]
    bounds = []
    for sl in heads:
        qf = q_ref[0, :, sl].astype(F32)
        bounds.append(jnp.sqrt(jnp.sum(qf * qf, axis=-1, keepdims=True)) * kmax * BOUND_SLACK)
    worst = functools.reduce(jnp.maximum, [jnp.max(bd) for bd in bounds])
    bounded = 2.0 * worst <= SAFE_EXP2_RANGE

    def key_chunk(c, tk):
        start = pl.multiple_of(c * tk, tk)
        return kt_ref[0, 0, :, pl.ds(start, tk)], v_ref[0, pl.ds(start, tk), :]

    @pl.when(bounded)
    def _():
        def chunk(c, carry):
            kt, v = key_chunk(c, tk_bounded)
            out = []
            for sl, m, (l, acc) in zip(heads, bounds, carry):
                s = jnp.dot(q_ref[0, :, sl], kt, preferred_element_type=F32)
                p = jnp.exp2(s - m)
                l = l + jnp.sum(p, axis=-1, keepdims=True)
                acc = acc + jnp.dot(p.astype(BF16), v, preferred_element_type=F32)
                out.append((l, acc))
            return tuple(out)

        init = tuple((jnp.zeros((tq, 1), F32), jnp.zeros((tq, HEAD_DIM), F32)) for _ in heads)
        final = lax.fori_loop(0, n_keys // tk_bounded, chunk, init)
        for sl, (l, acc) in zip(heads, final):
            o_ref[0, :, sl] = acc / l

    @pl.when(jnp.logical_not(bounded))
    def _():
        def chunk(c, carry):
            kt, v = key_chunk(c, tk_online)
            out = []
            for sl, (m, l, acc) in zip(heads, carry):
                s = jnp.dot(q_ref[0, :, sl], kt, preferred_element_type=F32)
                m_new = jnp.maximum(m, jnp.max(s, axis=-1, keepdims=True))
                p = jnp.exp2(s - m_new)
                alpha = jnp.exp2(m - m_new)
                l = alpha * l + jnp.sum(p, axis=-1, keepdims=True)
                acc = alpha * acc + jnp.dot(p.astype(BF16), v, preferred_element_type=F32)
                out.append((m_new, l, acc))
            return tuple(out)

        init = tuple((jnp.full((tq, 1), -jnp.inf, F32), jnp.zeros((tq, 1), F32),
                      jnp.zeros((tq, HEAD_DIM), F32)) for _ in heads)
        final = lax.fori_loop(0, n_keys // tk_online, chunk, init)
        for sl, (_, l, acc) in zip(heads, final):
            o_ref[0, :, sl] = acc / l


def _attention(q, kt, v, *, tq=1024, tk_bounded=4096, tk_online=2048):
    b, s, _ = q.shape
    tq = _tile(s, tq)
    tk_bounded = _tile(s, tk_bounded)
    tk_online = _tile(s, tk_online)
    gw = Q_PER_KV * HEAD_DIM
    return pl.pallas_call(
        functools.partial(_attn_kernel, tk_bounded=tk_bounded, tk_online=tk_online),
        out_shape=jax.ShapeDtypeStruct((b, s, ATTN_WIDTH), F32),
        grid=(b, N_KV_HEADS, s // tq),
        in_specs=[
            pl.BlockSpec((1, tq, gw), lambda i, g, j: (i, j, g)),
            pl.BlockSpec((1, 1, HEAD_DIM, s), lambda i, g, j: (i, g, 0, 0),
                         pipeline_mode=pl.Buffered(1)),
            pl.BlockSpec((1, s, HEAD_DIM), lambda i, g, j: (i, 0, g),
                         pipeline_mode=pl.Buffered(1)),
        ],
        out_specs=pl.BlockSpec((1, tq, gw), lambda i, g, j: (i, j, g)),
        scratch_shapes=[pltpu.VMEM((1, HEAD_DIM), F32)],
        compiler_params=pltpu.CompilerParams(
            dimension_semantics=("parallel", "parallel", "arbitrary"),
            vmem_limit_bytes=VMEM_LIMIT_BYTES),
        name="attention",
    )(q, kt, v)


def _conv(x_ref, prev_ref, next_ref, wc_ref, bc_ref, ext_ref, xc_ref):
    t = pl.program_id(1)
    nt = pl.num_programs(1)
    tt = x_ref.shape[1]
    ext_ref[0:SUBLANES] = jnp.where(t > 0, prev_ref[0], 0.0)
    ext_ref[SUBLANES:SUBLANES + tt] = x_ref[0]
    ext_ref[SUBLANES + tt:] = jnp.where(t < nt - 1, next_ref[0], 0.0)
    for n in range(RNN_BLOCKS):
        sl = slice(n * RNN_BLOCK_W, (n + 1) * RNN_BLOCK_W)
        ext = ext_ref[:, sl]
        xc = bc_ref[:, sl]
        for j in range(CONV_W):
            shift = (CONV_PAD_L - j) % ext.shape[0]
            tap = pltpu.roll(ext, shift, axis=0) if shift else ext
            xc = xc + tap[SUBLANES:SUBLANES + tt] * wc_ref[j:j + 1, sl]
        xc_ref[0, :, sl] = xc


def _gates(xc_ref, wg_ref, ba_ref, bx_ref, lam_ref, a_ref, u_ref):
    nl = -lam_ref[...]
    half_decay = (-0.5 * RG_C * LOG2E) * (jnp.maximum(nl, 0.0)
                                          + jnp.log(1.0 + jnp.exp(-jnp.abs(nl))))
    half_ba = 0.5 * ba_ref[...]
    half_bx = 0.5 * bx_ref[...]
    for n in range(RNN_BLOCKS):
        sl = slice(n * RNN_BLOCK_W, (n + 1) * RNN_BLOCK_W)
        xc = xc_ref[0, :, sl]
        gates = jnp.dot(xc.astype(BF16), wg_ref[n], preferred_element_type=F32)
        tr = jnp.tanh(gates[:, :RNN_BLOCK_W] + half_ba[:, sl])
        i = 0.5 * jnp.tanh(gates[:, RNN_BLOCK_W:] + half_bx[:, sl]) + 0.5
        a = jnp.exp2(half_decay[:, sl] * tr + half_decay[:, sl])
        a_ref[:, sl] = a
        v = 1.0 - a * a
        u_ref[:, sl] = jnp.where(v > 0.0, v * lax.rsqrt(v), 0.0) * (i * xc)


def _rglru_fwd_kernel(x_ref, prev_ref, next_ref, wc_ref, bc_ref, wg_ref, ba_ref, bx_ref, lam_ref,
                      xc_ref, hf_ref, ext_ref, a_ref, u_ref, carry_ref):
    tt = x_ref.shape[1]

    @pl.when(pl.program_id(1) == 0)
    def _():
        carry_ref[...] = jnp.zeros_like(carry_ref)

    _conv(x_ref, prev_ref, next_ref, wc_ref, bc_ref, ext_ref, xc_ref)
    _gates(xc_ref, wg_ref, ba_ref, bx_ref, lam_ref, a_ref, u_ref)

    def step(i, h):
        h = a_ref[pl.ds(i, 1), :] * h + u_ref[pl.ds(i, 1), :]
        hf_ref[0, pl.ds(i, 1), :] = h
        return h

    carry_ref[...] = lax.fori_loop(0, tt, step, carry_ref[...], unroll=8)


def _rglru_bwd_kernel(xc_ref, wg_ref, ba_ref, bx_ref, lam_ref, hf_ref, y_ref,
                      o_ref, a_ref, u_ref, hb_ref, carry_ref):
    tt = xc_ref.shape[1]

    @pl.when(pl.program_id(1) == 0)
    def _():
        carry_ref[...] = jnp.zeros_like(carry_ref)

    _gates(xc_ref, wg_ref, ba_ref, bx_ref, lam_ref, a_ref, u_ref)

    def step(k, h):
        i = tt - 1 - k
        h = a_ref[pl.ds(i, 1), :] * h + u_ref[pl.ds(i, 1), :]
        hb_ref[pl.ds(i, 1), :] = h
        return h

    carry_ref[...] = lax.fori_loop(0, tt, step, carry_ref[...], unroll=8)

    y = y_ref[0]
    gelu = 0.5 * y * (1.0 + jnp.tanh(math.sqrt(2.0 / math.pi) * (y + 0.044715 * (y * y * y))))
    o_ref[0] = (hf_ref[0] + hb_ref[...]) * gelu


def _rglru(xr, yr, wc, bc, wg_f, ba_f, bx_f, lam_f, wg_b, ba_b, bx_b, lam_b, *, tt=1024):
    b, s, r = xr.shape
    tt = _tile(s, tt)
    nt = s // tt
    groups = tt // SUBLANES
    last_group = s // SUBLANES - 1
    gate_params = [_resident(wg_f.shape), _resident(ba_f.shape), _resident(bx_f.shape),
                   _resident(lam_f.shape)]
    chunk = (tt, r)
    cparams = pltpu.CompilerParams(dimension_semantics=("parallel", "arbitrary"),
                                   vmem_limit_bytes=VMEM_LIMIT_BYTES)
    full = jax.ShapeDtypeStruct((b, s, r), F32)

    cur = pl.BlockSpec((1, tt, r), lambda i, j: (i, j, 0))
    prev = pl.BlockSpec((1, SUBLANES, r), lambda i, j: (i, jnp.maximum(j * groups - 1, 0), 0))
    nxt = pl.BlockSpec((1, SUBLANES, r),
                       lambda i, j: (i, jnp.minimum((j + 1) * groups, last_group), 0))
    xc, hf = pl.pallas_call(
        _rglru_fwd_kernel,
        out_shape=(full, full),
        grid=(b, nt),
        in_specs=[cur, prev, nxt, _resident(wc.shape), _resident(bc.shape)] + gate_params,
        out_specs=(cur, cur),
        scratch_shapes=[pltpu.VMEM((tt + 2 * SUBLANES, r), F32), pltpu.VMEM(chunk, F32),
                        pltpu.VMEM(chunk, F32), pltpu.VMEM((1, r), F32)],
        compiler_params=cparams,
        name="rglru_fwd",
    )(xr, xr, xr, wc, bc, wg_f, ba_f, bx_f, lam_f)

    rev = pl.BlockSpec((1, tt, r), lambda i, j: (i, nt - 1 - j, 0))
    return pl.pallas_call(
        _rglru_bwd_kernel,
        out_shape=full,
        grid=(b, nt),
        in_specs=[rev] + gate_params + [rev, rev],
        out_specs=rev,
        scratch_shapes=[pltpu.VMEM(chunk, F32), pltpu.VMEM(chunk, F32), pltpu.VMEM(chunk, F32),
                        pltpu.VMEM((1, r), F32)],
        compiler_params=cparams,
        name="rglru_bwd",
    )(xc, wg_b, ba_b, bx_b, lam_b, hf, yr)


def _out_proj_kernel(x_ref, a_ref, r_ref, ga_ref, gr_ref, wa_ref, wr_ref, gp_ref, o_ref):
    an = _rms(a_ref[...], ga_ref[...]).astype(BF16)
    rn = _rms(r_ref[...], gr_ref[...]).astype(BF16)
    y = (jnp.dot(an, wa_ref[...], preferred_element_type=F32)
         + jnp.dot(rn, wr_ref[...], preferred_element_type=F32))
    o_ref[...] = x_ref[...] + _rms(y, gp_ref[...])


def _out_proj(x, attn, rnn, ga, gr, wa, wr, gp, *, tm=512):
    t, d = x.shape
    tm = _tile(t, tm)
    tok = lambda w: pl.BlockSpec((tm, w), lambda i: (i, 0))
    return pl.pallas_call(
        _out_proj_kernel,
        out_shape=jax.ShapeDtypeStruct((t, d), F32),
        grid=(t // tm,),
        in_specs=[tok(d), tok(ATTN_WIDTH), tok(RNN_WIDTH),
                  _resident(ga.shape), _resident(gr.shape),
                  _resident(wa.shape), _resident(wr.shape), _resident(gp.shape)],
        out_specs=tok(d),
        compiler_params=pltpu.CompilerParams(
            dimension_semantics=("parallel",),
            vmem_limit_bytes=VMEM_LIMIT_BYTES),
        name="out_proj",
    )(x, attn, rnn, ga, gr, wa, wr, gp)


def _rope_tables(s):
    n_rows = s // GRID_W
    lane = jnp.arange(HEAD_DIM)
    inv_freq = ROPE_THETA ** (-jnp.arange(ROPE_PAIRS, dtype=F32) / ROPE_PAIRS)
    inv_lane = inv_freq[lane % ROPE_PAIRS]
    uses_col = lane >= 2 * ROPE_PAIRS
    upper = (lane % (2 * ROPE_PAIRS)) >= ROPE_PAIRS
    ang_r = (jnp.arange(n_rows, dtype=F32)[:, None] * inv_lane)[:, None, :]
    ang_c = (jnp.arange(GRID_W, dtype=F32)[:, None] * inv_lane)[None, :, :]
    cos = jnp.where(uses_col, jnp.cos(ang_c), jnp.cos(ang_r)).reshape(s, HEAD_DIM)
    sin = jnp.where(uses_col, jnp.sin(ang_c), jnp.sin(ang_r)).reshape(s, HEAD_DIM)
    sin_lo = jnp.where(upper, 0.0, -sin)
    sin_hi = jnp.where(upper, sin, 0.0)
    return cos, sin_lo, sin_hi


def _prepare(p):
    row = lambda v: v.reshape(1, -1)
    w_in = p["w_in"].astype(BF16)
    c0, c1, c2, c3 = ATTN_WIDTH, ATTN_WIDTH + KV_WIDTH, ATTN_WIDTH + 2 * KV_WIDTH, \
        ATTN_WIDTH + 2 * KV_WIDTH + RNN_WIDTH
    w_out = p["w_out"].astype(BF16)
    out = dict(
        ffn1=(row(p["g_ffn1_pre"]), p["w_ffn1_gate"].astype(BF16), p["w_ffn1_up"].astype(BF16),
              p["w_ffn1_down"].astype(BF16), row(p["g_ffn1_post"])),
        ffn2=(row(p["g_ffn2_pre"]), p["w_ffn2_gate"].astype(BF16), p["w_ffn2_up"].astype(BF16),
              p["w_ffn2_down"].astype(BF16), row(p["g_ffn2_post"])),
        in_proj=(row(p["g_mix_pre"]), w_in[:, :c0], w_in[:, c0:c1], w_in[:, c1:c2],
                 w_in[:, c2:c3], w_in[:, c3:], row(p["g_q"]), row(p["g_k"])),
        rglru=(p["w_conv"], row(p["b_conv"]),
               (0.5 * jnp.concatenate([p["w_a_fwd"], p["w_x_fwd"]], axis=-1)).astype(BF16),
               row(p["b_a_fwd"]), row(p["b_x_fwd"]), row(p["lam_fwd"]),
               (0.5 * jnp.concatenate([p["w_a_bwd"], p["w_x_bwd"]], axis=-1)).astype(BF16),
               row(p["b_a_bwd"]), row(p["b_x_bwd"]), row(p["lam_bwd"])),
        out_proj=(row(p["g_attn_out"]), row(p["g_rnn_out"]), w_out[:ATTN_WIDTH], w_out[ATTN_WIDTH:],
                  row(p["g_mix_post"])),
    )
    return out


def _layer(x, w, rope):
    b, s, d = x.shape
    x1 = _ffn(x.reshape(b * s, d), *w["ffn1"])
    q, kt, v, xr, yr = _in_proj(x1.reshape(b, s, d), *w["in_proj"], *rope)
    attn = _attention(q, kt, v)
    rnn = _rglru(xr, yr, *w["rglru"])
    x2 = _out_proj(x1, attn.reshape(b * s, -1), rnn.reshape(b * s, -1), *w["out_proj"])
    x3 = _ffn(x2, *w["ffn2"])
    return x3.reshape(b, s, d)


_PARAM_NAMES = (
    "g_ffn1_pre", "w_ffn1_gate", "w_ffn1_up", "w_ffn1_down", "g_ffn1_post",
    "g_mix_pre", "w_in", "g_q", "g_k", "w_conv", "b_conv",
    "w_a_fwd", "b_a_fwd", "w_x_fwd", "b_x_fwd", "lam_fwd",
    "w_a_bwd", "b_a_bwd", "w_x_bwd", "b_x_bwd", "lam_bwd",
    "g_attn_out", "g_rnn_out", "w_out", "g_mix_post",
    "g_ffn2_pre", "w_ffn2_gate", "w_ffn2_up", "w_ffn2_down", "g_ffn2_post")


def kernel(x_prompt, x_sample, g_ffn1_pre, w_ffn1_gate, w_ffn1_up, w_ffn1_down, g_ffn1_post, g_mix_pre, w_in, g_q, g_k, w_conv, b_conv, w_a_fwd, b_a_fwd, w_x_fwd, b_x_fwd, lam_fwd, w_a_bwd, b_a_bwd, w_x_bwd, b_x_bwd, lam_bwd, g_attn_out, g_rnn_out, w_out, g_mix_post, g_ffn2_pre, w_ffn2_gate, w_ffn2_up, w_ffn2_down, g_ffn2_post):
    stacked = (g_ffn1_pre, w_ffn1_gate, w_ffn1_up, w_ffn1_down, g_ffn1_post, g_mix_pre, w_in, g_q,
               g_k, w_conv, b_conv, w_a_fwd, b_a_fwd, w_x_fwd, b_x_fwd, lam_fwd, w_a_bwd, b_a_bwd,
               w_x_bwd, b_x_bwd, lam_bwd, g_attn_out, g_rnn_out, w_out, g_mix_post, g_ffn2_pre,
               w_ffn2_gate, w_ffn2_up, w_ffn2_down, g_ffn2_post)
    depth = g_ffn1_pre.shape[0]
    y_prompt, y_sample = x_prompt, x_sample
    rope = _rope_tables(max(x_prompt.shape[1], x_sample.shape[1]))
    for layer in range(depth):
        w = _prepare({n: a[layer] for n, a in zip(_PARAM_NAMES, stacked)})
        y_prompt = _layer(y_prompt, w, rope)
        y_sample = _layer(y_sample, w, rope)
    return (y_prompt, y_sample)
```

```python
import functools
import math

import jax
import jax.numpy as jnp
from jax import lax
from jax.experimental import pallas as pl
from jax.experimental.pallas import tpu as pltpu

F32 = jnp.float32
BF16 = jnp.bfloat16

EPS = 1e-6
HEAD_DIM = 128
N_Q_HEADS = 8
N_KV_HEADS = 2
Q_PER_KV = N_Q_HEADS // N_KV_HEADS
ATTN_WIDTH = N_Q_HEADS * HEAD_DIM
KV_WIDTH = N_KV_HEADS * HEAD_DIM
RNN_BLOCKS = 8
RNN_BLOCK_W = 128
RNN_WIDTH = RNN_BLOCKS * RNN_BLOCK_W
GRID_W = 64
ROPE_PAIRS = HEAD_DIM // 4
ROPE_THETA = 10000.0
CONV_W = 4
CONV_PAD_L = 2
RG_C = 8.0
LOG2E = math.log2(math.e)
SUBLANES = 8
SCAN_GROUP = 16
RNN_BATCH_ROWS = 2
NORM_ROWS = 32
SAFE_EXP2_RANGE = 100.0
BOUND_SLACK = 1.0 + 2.0 ** -10

VMEM_LIMIT_BYTES = 58 * 1024 * 1024


def _tile(n, want):
    t = min(n, want)
    assert n % t == 0, (n, want)
    return t


def _rms(x, g):
    ms = jnp.mean(x * x, axis=-1, keepdims=True)
    return x * lax.rsqrt(ms + EPS) * g


def _silu(x):
    h = 0.5 * x
    return h * jnp.tanh(h) + h


def _resident(shape):
    nd = len(shape)
    return pl.BlockSpec(shape, lambda *_: (0,) * nd, pipeline_mode=pl.Buffered(1))


def _ffn_kernel(x_ref, gpre_ref, wg_ref, wu_ref, wd_ref, gpost_ref, o_ref, h_ref, *, row_split):
    j = pl.program_id(1)
    last = pl.num_programs(1) - 1
    tm, _ = x_ref.shape
    hm = tm // row_split

    def step(is_first, is_last):
        gp = 0.5 * gpost_ref[...]
        for r in range(row_split):
            rows = slice(r * hm, (r + 1) * hm)
            chunks = [slice(r * hm + i * NORM_ROWS, r * hm + (i + 1) * NORM_ROWS)
                      for i in range(hm // NORM_ROWS)]
            if is_first:
                for c in chunks:
                    h_ref[c, :] = _rms(x_ref[c, :], gpre_ref[...]).astype(BF16)
            h = h_ref[rows, :]
            g = jnp.dot(h, wg_ref[...], preferred_element_type=F32)
            u = jnp.dot(h, wu_ref[...], preferred_element_type=F32)
            a = (_silu(g) * u).astype(BF16)
            down = jnp.dot(a, wd_ref[...], preferred_element_type=F32)
            if is_first:
                o_ref[rows, :] = down
            else:
                o_ref[rows, :] += down
            if is_last:
                for c in chunks:
                    o_ref[c, :] = x_ref[c, :] + _rms(o_ref[c, :], gp)

    pl.when(j == 0)(lambda: step(True, False))
    pl.when(jnp.logical_and(j > 0, j < last))(lambda: step(False, False))
    pl.when(j == last)(lambda: step(False, True))


def _ffn(x, g_pre, wg, wu, wd, g_post, *, tm=1024, tf=512):
    t, d = x.shape
    f = wg.shape[1]
    tm = _tile(t, tm)
    tf = _tile(f, tf)
    assert f // tf >= 2, "first and last d_ff steps must be distinct grid steps"
    row_split = 2 if tm % (2 * NORM_ROWS) == 0 else 1
    return pl.pallas_call(
        functools.partial(_ffn_kernel, row_split=row_split),
        out_shape=jax.ShapeDtypeStruct((t, d), F32),
        grid=(t // tm, f // tf),
        in_specs=[
            pl.BlockSpec((tm, d), lambda i, j: (i, 0)),
            pl.BlockSpec((1, d), lambda i, j: (0, 0)),
            pl.BlockSpec((d, tf), lambda i, j: (0, j)),
            pl.BlockSpec((d, tf), lambda i, j: (0, j)),
            pl.BlockSpec((tf, d), lambda i, j: (j, 0)),
            pl.BlockSpec((1, d), lambda i, j: (0, 0)),
        ],
        out_specs=pl.BlockSpec((tm, d), lambda i, j: (i, 0)),
        scratch_shapes=[pltpu.VMEM((tm, d), BF16)],
        compiler_params=pltpu.CompilerParams(
            dimension_semantics=("parallel", "arbitrary"),
            vmem_limit_bytes=VMEM_LIMIT_BYTES),
        name="ffn",
    )(x, g_pre, wg, wu, wd, g_post)


def _rope(xn, cos, sin_lo, sin_hi):
    return (xn * cos
            + pltpu.roll(xn, HEAD_DIM - ROPE_PAIRS, axis=1) * sin_lo
            + pltpu.roll(xn, ROPE_PAIRS, axis=1) * sin_hi)


def _in_proj_kernel(x_ref, g_ref, wq_ref, wk_ref, wv_ref, wx_ref, wy_ref, gq_ref, gk_ref,
                    cos_ref, slo_ref, shi_ref,
                    q_ref, kt_ref, v_ref, xr_ref, yr_ref):
    h = _rms(x_ref[0], g_ref[...]).astype(BF16)
    cos = cos_ref[...]
    slo = slo_ref[...]
    shi = shi_ref[...]

    q = jnp.dot(h, wq_ref[...], preferred_element_type=F32)
    gq = gq_ref[...] * (HEAD_DIM ** -0.5 * LOG2E)
    for hd in range(N_Q_HEADS):
        sl = slice(hd * HEAD_DIM, (hd + 1) * HEAD_DIM)
        q_ref[0, :, sl] = _rope(_rms(q[:, sl], gq), cos, slo, shi).astype(BF16)

    k = jnp.dot(h, wk_ref[...], preferred_element_type=F32)
    for hd in range(N_KV_HEADS):
        sl = slice(hd * HEAD_DIM, (hd + 1) * HEAD_DIM)
        kr = _rope(_rms(k[:, sl], gk_ref[...]), cos, slo, shi)
        kt_ref[0, hd] = kr.T.astype(BF16)

    v_ref[0] = jnp.dot(h, wv_ref[...], preferred_element_type=F32).astype(BF16)
    xr_ref[0] = jnp.dot(h, wx_ref[...], preferred_element_type=F32)
    yr_ref[0] = jnp.dot(h, wy_ref[...], preferred_element_type=F32)


def _in_proj(x, g, wq, wk, wv, wx, wy, gq, gk, cos, slo, shi, *, tm=512):
    b, s, d = x.shape
    tm = _tile(s, tm)
    tok = lambda w: pl.BlockSpec((1, tm, w), lambda i, j: (i, j, 0))
    tab = pl.BlockSpec((tm, HEAD_DIM), lambda i, j: (j, 0))
    return pl.pallas_call(
        _in_proj_kernel,
        out_shape=(
            jax.ShapeDtypeStruct((b, s, ATTN_WIDTH), BF16),
            jax.ShapeDtypeStruct((b, N_KV_HEADS, HEAD_DIM, s), BF16),
            jax.ShapeDtypeStruct((b, s, KV_WIDTH), BF16),
            jax.ShapeDtypeStruct((b, s, RNN_WIDTH), F32),
            jax.ShapeDtypeStruct((b, s, RNN_WIDTH), F32),
        ),
        grid=(b, s // tm),
        in_specs=[
            tok(d), _resident(g.shape),
            _resident(wq.shape), _resident(wk.shape), _resident(wv.shape),
            _resident(wx.shape), _resident(wy.shape),
            _resident(gq.shape), _resident(gk.shape),
            tab, tab, tab,
        ],
        out_specs=(
            tok(ATTN_WIDTH),
            pl.BlockSpec((1, N_KV_HEADS, HEAD_DIM, tm), lambda i, j: (i, 0, 0, j)),
            tok(KV_WIDTH), tok(RNN_WIDTH), tok(RNN_WIDTH),
        ),
        compiler_params=pltpu.CompilerParams(
            dimension_semantics=("parallel", "parallel"),
            vmem_limit_bytes=VMEM_LIMIT_BYTES),
        name="in_proj",
    )(x, g, wq, wk, wv, wx, wy, gq, gk, cos, slo, shi)


def _attn_kernel(q_ref, kt_ref, v_ref, o_ref, kmax_ref, *, tk_bounded, tk_online):
    tq = q_ref.shape[1]
    n_keys = kt_ref.shape[3]
    heads =[slice(hd * HEAD_DIM, (hd + 1) * HEAD_DIM) for hd in range(Q_PER_KV)]

    @pl.when(pl.program_id(2) == 0)
    def _():
        kf = kt_ref[0, 0].astype(F32)
        k2 = jnp.max(jnp.sum(kf * kf, axis=0, keepdims=True), axis=-1, keepdims=True)
        kmax_ref[...] = jnp.broadcast_to(jnp.sqrt(k2), kmax_ref.shape)

    kmax = kmax_ref[:, :1]
    bounds = []
    for sl in heads:
        qf = q_ref[0, :, sl].astype(F32)
        bounds.append(jnp.sqrt(jnp.sum(qf * qf, axis=-1, keepdims=True)) * kmax * BOUND_SLACK)
    worst = functools.reduce(jnp.maximum, [jnp.max(bd) for bd in bounds])
    bounded = 2.0 * worst <= SAFE_EXP2_RANGE

    def key_chunk(c, tk):
        start = pl.multiple_of(c * tk, tk)
        return kt_ref[0, 0, :, pl.ds(start, tk)], v_ref[0, pl.ds(start, tk), :]

    @pl.when(bounded)
    def _():
        def chunk(c, carry):
            kt, v = key_chunk(c, tk_bounded)
            out = []
            for sl, m, (l, acc) in zip(heads, bounds, carry):
                s = jnp.dot(q_ref[0, :, sl], kt, preferred_element_type=F32)
                p = jnp.exp2(s - m)
                l = l + jnp.sum(p, axis=-1, keepdims=True)
                acc = acc + jnp.dot(p.astype(BF16), v, preferred_element_type=F32)
                out.append((l, acc))
            return tuple(out)

        init = tuple((jnp.zeros((tq, 1), F32), jnp.zeros((tq, HEAD_DIM), F32)) for _ in heads)
        final = lax.fori_loop(0, n_keys // tk_bounded, chunk, init)
        for sl, (l, acc) in zip(heads, final):
            o_ref[0, :, sl] = acc / l

    @pl.when(jnp.logical_not(bounded))
    def _():
        def chunk(c, carry):
            kt, v = key_chunk(c, tk_online)
            out = []
            for sl, (m, l, acc) in zip(heads, carry):
                s = jnp.dot(q_ref[0, :, sl], kt, preferred_element_type=F32)
                m_new = jnp.maximum(m, jnp.max(s, axis=-1, keepdims=True))
                p = jnp.exp2(s - m_new)
                alpha = jnp.exp2(m - m_new)
                l = alpha * l + jnp.sum(p, axis=-1, keepdims=True)
                acc = alpha * acc + jnp.dot(p.astype(BF16), v, preferred_element_type=F32)
                out.append((m_new, l, acc))
            return tuple(out)

        init = tuple((jnp.full((tq, 1), -jnp.inf, F32), jnp.zeros((tq, 1), F32),
                      jnp.zeros((tq, HEAD_DIM), F32)) for _ in heads)
        final = lax.fori_loop(0, n_keys // tk_online, chunk, init)
        for sl, (_, l, acc) in zip(heads, final):
            o_ref[0, :, sl] = acc / l


def _attention(q, kt, v, *, tq=1024, tk_bounded=4096, tk_online=2048):
    b, s, _ = q.shape
    tq = _tile(s, tq)
    tk_bounded = _tile(s, tk_bounded)
    tk_online = _tile(s, tk_online)
    gw = Q_PER_KV * HEAD_DIM
    return pl.pallas_call(
        functools.partial(_attn_kernel, tk_bounded=tk_bounded, tk_online=tk_online),
        out_shape=jax.ShapeDtypeStruct((b, s, ATTN_WIDTH), F32),
        grid=(b, N_KV_HEADS, s // tq),
        in_specs=[
            pl.BlockSpec((1, tq, gw), lambda i, g, j: (i, j, g)),
            pl.BlockSpec((1, 1, HEAD_DIM, s), lambda i, g, j: (i, g, 0, 0),
                         pipeline_mode=pl.Buffered(1)),
            pl.BlockSpec((1, s, HEAD_DIM), lambda i, g, j: (i, 0, g),
                         pipeline_mode=pl.Buffered(1)),
        ],
        out_specs=pl.BlockSpec((1, tq, gw), lambda i, g, j: (i, j, g)),
        scratch_shapes=[pltpu.VMEM((1, HEAD_DIM), F32)],
        compiler_params=pltpu.CompilerParams(
            dimension_semantics=("parallel", "parallel", "arbitrary"),
            vmem_limit_bytes=VMEM_LIMIT_BYTES),
        name="attention",
    )(q, kt, v)


def _conv(row, x_ref, prev_ref, next_ref, wc_ref, bc_ref, ext_ref, xc_ref):
    t = pl.program_id(1)
    nt = pl.num_programs(1)
    tt = x_ref.shape[1]
    ext_ref[row, 0:SUBLANES] = jnp.where(t > 0, prev_ref[row], 0.0)
    ext_ref[row, SUBLANES:SUBLANES + tt] = x_ref[row]
    ext_ref[row, SUBLANES + tt:] = jnp.where(t < nt - 1, next_ref[row], 0.0)
    for n in range(RNN_BLOCKS):
        sl = slice(n * RNN_BLOCK_W, (n + 1) * RNN_BLOCK_W)
        ext = ext_ref[row, :, sl]
        xc = bc_ref[:, sl]
        for j in range(CONV_W):
            shift = (CONV_PAD_L - j) % ext.shape[0]
            tap = pltpu.roll(ext, shift, axis=0) if shift else ext
            xc = xc + tap[SUBLANES:SUBLANES + tt] * wc_ref[j:j + 1, sl]
        xc_ref[row, :, sl] = xc


def _gates(row, xc_ref, wg_ref, ba_ref, bx_ref, lam_ref, a_ref, u_ref):
    nl = -lam_ref[...]
    half_decay = (-0.5 * RG_C * LOG2E) * (jnp.maximum(nl, 0.0)
                                          + jnp.log(1.0 + jnp.exp(-jnp.abs(nl))))
    half_ba = 0.5 * ba_ref[...]
    half_bx = 0.5 * bx_ref[...]
    for n in range(RNN_BLOCKS):
        sl = slice(n * RNN_BLOCK_W, (n + 1) * RNN_BLOCK_W)
        xc = xc_ref[row, :, sl]
        gates = jnp.dot(xc.astype(BF16), wg_ref[n], preferred_element_type=F32)
        tr = jnp.tanh(gates[:, :RNN_BLOCK_W] + half_ba[:, sl])
        i = 0.5 * jnp.tanh(gates[:, RNN_BLOCK_W:] + half_bx[:, sl]) + 0.5
        a = jnp.exp2(half_decay[:, sl] * tr + half_decay[:, sl])
        a_ref[row, :, sl] = a
        v = 1.0 - a * a
        u_ref[row, :, sl] = jnp.where(v > 0.0, v * lax.rsqrt(v), 0.0) * (i * xc)


def _scan(a_ref, u_ref, o_ref, carry_ref, *, reverse):
    nb, tt, _ = a_ref.shape
    n_groups = tt // SCAN_GROUP

    def group(g, hs):
        first = (n_groups - 1 - g) * SCAN_GROUP if reverse else g * SCAN_GROUP
        rows = pl.ds(pl.multiple_of(first, SCAN_GROUP), SCAN_GROUP)
        views = [(a_ref.at[b, rows, :], u_ref.at[b, rows, :], o_ref.at[b, rows, :])
                 for b in range(nb)]
        hs = list(hs)
        for k in (reversed(range(SCAN_GROUP)) if reverse else range(SCAN_GROUP)):
            for b, (a_t, u_t, o_t) in enumerate(views):
                hs[b] = a_t[k:k + 1, :] * hs[b] + u_t[k:k + 1, :]
                o_t[k:k + 1, :] = hs[b]
        return tuple(hs)

    hs = lax.fori_loop(0, n_groups, group, tuple(carry_ref[b] for b in range(nb)))
    for b in range(nb):
        carry_ref[b] = hs[b]


def _rglru_fwd_kernel(x_ref, prev_ref, next_ref, wc_ref, bc_ref, wg_ref, ba_ref, bx_ref, lam_ref,
                      xc_ref, hf_ref, ext_ref, a_ref, u_ref, carry_ref):
    @pl.when(pl.program_id(1) == 0)
    def _():
        carry_ref[...] = jnp.zeros_like(carry_ref)

    for row in range(x_ref.shape[0]):
        _conv(row, x_ref, prev_ref, next_ref, wc_ref, bc_ref, ext_ref, xc_ref)
        _gates(row, xc_ref, wg_ref, ba_ref, bx_ref, lam_ref, a_ref, u_ref)
    _scan(a_ref, u_ref, hf_ref, carry_ref, reverse=False)


def _rglru_bwd_kernel(xc_ref, wg_ref, ba_ref, bx_ref, lam_ref, hf_ref, y_ref,
                      o_ref, a_ref, u_ref, hb_ref, carry_ref):
    @pl.when(pl.program_id(1) == 0)
    def _():
        carry_ref[...] = jnp.zeros_like(carry_ref)

    for row in range(xc_ref.shape[0]):
        _gates(row, xc_ref, wg_ref, ba_ref, bx_ref, lam_ref, a_ref, u_ref)
    _scan(a_ref, u_ref, hb_ref, carry_ref, reverse=True)

    for row in range(xc_ref.shape[0]):
        y = y_ref[row]
        gelu = 0.5 * y * (1.0 + jnp.tanh(math.sqrt(2.0 / math.pi) * (y + 0.044715 * (y * y * y))))
        o_ref[row] = (hf_ref[row] + hb_ref[row]) * gelu


def _rglru(xr, yr, wc, bc, wg_f, ba_f, bx_f, lam_f, wg_b, ba_b, bx_b, lam_b, *, tt=512):
    b, s, r = xr.shape
    tt = _tile(s, tt)
    nt = s // tt
    nb = _tile(b, RNN_BATCH_ROWS)
    groups = tt // SUBLANES
    last_group = s // SUBLANES - 1
    gate_params = [_resident(wg_f.shape), _resident(ba_f.shape), _resident(bx_f.shape),
                   _resident(lam_f.shape)]
    chunk = (nb, tt, r)
    cparams = pltpu.CompilerParams(dimension_semantics=("parallel", "arbitrary"),
                                   vmem_limit_bytes=VMEM_LIMIT_BYTES)
    full = jax.ShapeDtypeStruct((b, s, r), F32)

    cur = pl.BlockSpec(chunk, lambda i, j: (i, j, 0))
    prev = pl.BlockSpec((nb, SUBLANES, r), lambda i, j: (i, jnp.maximum(j * groups - 1, 0), 0))
    nxt = pl.BlockSpec((nb, SUBLANES, r),
                       lambda i, j: (i, jnp.minimum((j + 1) * groups, last_group), 0))
    xc, hf = pl.pallas_call(
        _rglru_fwd_kernel,
        out_shape=(full, full),
        grid=(b // nb, nt),
        in_specs=[cur, prev, nxt, _resident(wc.shape), _resident(bc.shape)] + gate_params,
        out_specs=(cur, cur),
        scratch_shapes=[pltpu.VMEM((nb, tt + 2 * SUBLANES, r), F32), pltpu.VMEM(chunk, F32),
                        pltpu.VMEM(chunk, F32), pltpu.VMEM((nb, 1, r), F32)],
        compiler_params=cparams,
        name="rglru_fwd",
    )(xr, xr, xr, wc, bc, wg_f, ba_f, bx_f, lam_f)

    rev = pl.BlockSpec(chunk, lambda i, j: (i, nt - 1 - j, 0))
    return pl.pallas_call(
        _rglru_bwd_kernel,
        out_shape=full,
        grid=(b // nb, nt),
        in_specs=[rev] + gate_params + [rev, rev],
        out_specs=rev,
        scratch_shapes=[pltpu.VMEM(chunk, F32), pltpu.VMEM(chunk, F32), pltpu.VMEM(chunk, F32),
                        pltpu.VMEM((nb, 1, r), F32)],
        compiler_params=cparams,
        name="rglru_bwd",
    )(xc, wg_b, ba_b, bx_b, lam_b, hf, yr)


def _out_proj_kernel(x_ref, a_ref, r_ref, ga_ref, gr_ref, wa_ref, wr_ref, gp_ref, o_ref):
    an = _rms(a_ref[...], ga_ref[...]).astype(BF16)
    rn = _rms(r_ref[...], gr_ref[...]).astype(BF16)
    y = (jnp.dot(an, wa_ref[...], preferred_element_type=F32)
         + jnp.dot(rn, wr_ref[...], preferred_element_type=F32))
    o_ref[...] = x_ref[...] + _rms(y, gp_ref[...])


def _out_proj(x, attn, rnn, ga, gr, wa, wr, gp, *, tm=512):
    t, d = x.shape
    tm = _tile(t, tm)
    tok = lambda w: pl.BlockSpec((tm, w), lambda i: (i, 0))
    return pl.pallas_call(
        _out_proj_kernel,
        out_shape=jax.ShapeDtypeStruct((t, d), F32),
        grid=(t // tm,),
        in_specs=[tok(d), tok(ATTN_WIDTH), tok(RNN_WIDTH),
                  _resident(ga.shape), _resident(gr.shape),
                  _resident(wa.shape), _resident(wr.shape), _resident(gp.shape)],
        out_specs=tok(d),
        compiler_params=pltpu.CompilerParams(
            dimension_semantics=("parallel",),
            vmem_limit_bytes=VMEM_LIMIT_BYTES),
        name="out_proj",
    )(x, attn, rnn, ga, gr, wa, wr, gp)


def _rope_tables(s):
    n_rows = s // GRID_W
    lane = jnp.arange(HEAD_DIM)
    inv_freq = ROPE_THETA ** (-jnp.arange(ROPE_PAIRS, dtype=F32) / ROPE_PAIRS)
    inv_lane = inv_freq[lane % ROPE_PAIRS]
    uses_col = lane >= 2 * ROPE_PAIRS
    upper = (lane % (2 * ROPE_PAIRS)) >= ROPE_PAIRS
    ang_r = (jnp.arange(n_rows, dtype=F32)[:, None] * inv_lane)[:, None, :]
    ang_c = (jnp.arange(GRID_W, dtype=F32)[:, None] * inv_lane)[None, :, :]
    cos = jnp.where(uses_col, jnp.cos(ang_c), jnp.cos(ang_r)).reshape(s, HEAD_DIM)
    sin = jnp.where(uses_col, jnp.sin(ang_c), jnp.sin(ang_r)).reshape(s, HEAD_DIM)
    sin_lo = jnp.where(upper, 0.0, -sin)
    sin_hi = jnp.where(upper, sin, 0.0)
    return cos, sin_lo, sin_hi


def _prepare(p):
    row = lambda v: v.reshape(1, -1)
    w_in = p["w_in"].astype(BF16)
    c0, c1, c2, c3 = ATTN_WIDTH, ATTN_WIDTH + KV_WIDTH, ATTN_WIDTH + 2 * KV_WIDTH, \
        ATTN_WIDTH + 2 * KV_WIDTH + RNN_WIDTH
    w_out = p["w_out"].astype(BF16)
    out = dict(
        ffn1=(row(p["g_ffn1_pre"]), p["w_ffn1_gate"].astype(BF16), p["w_ffn1_up"].astype(BF16),
              p["w_ffn1_down"].astype(BF16), row(p["g_ffn1_post"])),
        ffn2=(row(p["g_ffn2_pre"]), p["w_ffn2_gate"].astype(BF16), p["w_ffn2_up"].astype(BF16),
              p["w_ffn2_down"].astype(BF16), row(p["g_ffn2_post"])),
        in_proj=(row(p["g_mix_pre"]), w_in[:, :c0], w_in[:, c0:c1], w_in[:, c1:c2],
                 w_in[:, c2:c3], w_in[:, c3:], row(p["g_q"]), row(p["g_k"])),
        rglru=(p["w_conv"], row(p["b_conv"]),
               (0.5 * jnp.concatenate([p["w_a_fwd"], p["w_x_fwd"]], axis=-1)).astype(BF16),
               row(p["b_a_fwd"]), row(p["b_x_fwd"]), row(p["lam_fwd"]),
               (0.5 * jnp.concatenate([p["w_a_bwd"], p["w_x_bwd"]], axis=-1)).astype(BF16),
               row(p["b_a_bwd"]), row(p["b_x_bwd"]), row(p["lam_bwd"])),
        out_proj=(row(p["g_attn_out"]), row(p["g_rnn_out"]), w_out[:ATTN_WIDTH], w_out[ATTN_WIDTH:],
                  row(p["g_mix_post"])),
    )
    return out


def _layer(x, w, rope):
    b, s, d = x.shape
    x1 = _ffn(x.reshape(b * s, d), *w["ffn1"])
    q, kt, v, xr, yr = _in_proj(x1.reshape(b, s, d), *w["in_proj"], *rope)
    attn = _attention(q, kt, v)
    rnn = _rglru(xr, yr, *w["rglru"])
    x2 = _out_proj(x1, attn.reshape(b * s, -1), rnn.reshape(b * s, -1), *w["out_proj"])
    x3 = _ffn(x2, *w["ffn2"])
    return x3.reshape(b, s, d)


_PARAM_NAMES = (
    "g_ffn1_pre", "w_ffn1_gate", "w_ffn1_up", "w_ffn1_down", "g_ffn1_post",
    "g_mix_pre", "w_in", "g_q", "g_k", "w_conv", "b_conv",
    "w_a_fwd", "b_a_fwd", "w_x_fwd", "b_x_fwd", "lam_fwd",
    "w_a_bwd", "b_a_bwd", "w_x_bwd", "b_x_bwd", "lam_bwd",
    "g_attn_out", "g_rnn_out", "w_out", "g_mix_post",
    "g_ffn2_pre", "w_ffn2_gate", "w_ffn2_up", "w_ffn2_down", "g_ffn2_post")


def kernel(x_prompt, x_sample, g_ffn1_pre, w_ffn1_gate, w_ffn1_up, w_ffn1_down, g_ffn1_post, g_mix_pre, w_in, g_q, g_k, w_conv, b_conv, w_a_fwd, b_a_fwd, w_x_fwd, b_x_fwd, lam_fwd, w_a_bwd, b_a_bwd, w_x_bwd, b_x_bwd, lam_bwd, g_attn_out, g_rnn_out, w_out, g_mix_post, g_ffn2_pre, w_ffn2_gate, w_ffn2_up, w_ffn2_down, g_ffn2_post):
    stacked = (g_ffn1_pre, w_ffn1_gate, w_ffn1_up, w_ffn1_down, g_ffn1_post, g_mix_pre, w_in, g_q,
               g_k, w_conv, b_conv, w_a_fwd, b_a_fwd, w_x_fwd, b_x_fwd, lam_fwd, w_a_bwd, b_a_bwd,
               w_x_bwd, b_x_bwd, lam_bwd, g_attn_out, g_rnn_out, w_out, g_mix_post, g_ffn2_pre,
               w_ffn2_gate, w_ffn2_up, w_ffn2_down, g_ffn2_post)
    depth = g_ffn1_pre.shape[0]
    y_prompt, y_sample = x_prompt, x_sample
    rope = _rope_tables(max(x_prompt.shape[1], x_sample.shape[1]))
    for layer in range(depth):
        w = _prepare({n: a[layer] for n, a in zip(_PARAM_NAMES, stacked)})
        y_prompt = _layer(y_prompt, w, rope)
        y_sample = _layer(y_sample, w, rope)
    return (y_prompt, y_sample)
```

```python
import functools
import math

import jax
import jax.numpy as jnp
from jax import lax
from jax.experimental import pallas as pl
from jax.experimental.pallas import tpu as pltpu

F32 = jnp.float32
BF16 = jnp.bfloat16

EPS = 1e-6
HEAD_DIM = 128
N_Q_HEADS = 8
N_KV_HEADS = 2
Q_PER_KV = N_Q_HEADS // N_KV_HEADS
ATTN_WIDTH = N_Q_HEADS * HEAD_DIM
KV_WIDTH = N_KV_HEADS * HEAD_DIM
RNN_BLOCKS = 8
RNN_BLOCK_W = 128
RNN_WIDTH = RNN_BLOCKS * RNN_BLOCK_W
GRID_W = 64
ROPE_PAIRS = HEAD_DIM // 4
ROPE_THETA = 10000.0
CONV_W = 4
CONV_PAD_L = 2
RG_C = 8.0
LOG2E = math.log2(math.e)
SUBLANES = 8
SCAN_GROUP = 16
RNN_BATCH_ROWS = 2
NORM_ROWS = 32
SAFE_EXP2_RANGE = 100.0
BOUND_SLACK = 1.0 + 2.0 ** -6

VMEM_LIMIT_BYTES = 58 * 1024 * 1024


def _tile(n, want):
    t = min(n, want)
    assert n % t == 0, (n, want)
    return t


def _rms(x, g):
    ms = jnp.mean(x * x, axis=-1, keepdims=True)
    return x * lax.rsqrt(ms + EPS) * g


def _silu(x):
    h = 0.5 * x
    return h * jnp.tanh(h) + h


def _resident(shape):
    nd = len(shape)
    return pl.BlockSpec(shape, lambda *_: (0,) * nd, pipeline_mode=pl.Buffered(1))


def _ffn_kernel(x_ref, gpre_ref, wg_ref, wu_ref, wd_ref, gpost_ref, o_ref, h_ref, *, row_split):
    j = pl.program_id(1)
    last = pl.num_programs(1) - 1
    tm, _ = x_ref.shape
    hm = tm // row_split

    def step(is_first, is_last):
        gp = 0.5 * gpost_ref[...]
        for r in range(row_split):
            rows = slice(r * hm, (r + 1) * hm)
            chunks = [slice(r * hm + i * NORM_ROWS, r * hm + (i + 1) * NORM_ROWS)
                      for i in range(hm // NORM_ROWS)]
            if is_first:
                for c in chunks:
                    h_ref[c, :] = _rms(x_ref[c, :], gpre_ref[...]).astype(BF16)
            h = h_ref[rows, :]
            g = jnp.dot(h, wg_ref[...], preferred_element_type=F32)
            u = jnp.dot(h, wu_ref[...], preferred_element_type=F32)
            a = (_silu(g) * u).astype(BF16)
            down = jnp.dot(a, wd_ref[...], preferred_element_type=F32)
            if is_first:
                o_ref[rows, :] = down
            else:
                o_ref[rows, :] += down
            if is_last:
                for c in chunks:
                    o_ref[c, :] = x_ref[c, :] + _rms(o_ref[c, :], gp)

    pl.when(j == 0)(lambda: step(True, False))
    pl.when(jnp.logical_and(j > 0, j < last))(lambda: step(False, False))
    pl.when(j == last)(lambda: step(False, True))


def _ffn(x, g_pre, wg, wu, wd, g_post, *, tm=1024, tf=512):
    t, d = x.shape
    f = wg.shape[1]
    tm = _tile(t, tm)
    tf = _tile(f, tf)
    assert f // tf >= 2, "first and last d_ff steps must be distinct grid steps"
    row_split = 2 if tm % (2 * NORM_ROWS) == 0 else 1
    return pl.pallas_call(
        functools.partial(_ffn_kernel, row_split=row_split),
        out_shape=jax.ShapeDtypeStruct((t, d), F32),
        grid=(t // tm, f // tf),
        in_specs=[
            pl.BlockSpec((tm, d), lambda i, j: (i, 0)),
            pl.BlockSpec((1, d), lambda i, j: (0, 0)),
            pl.BlockSpec((d, tf), lambda i, j: (0, j)),
            pl.BlockSpec((d, tf), lambda i, j: (0, j)),
            pl.BlockSpec((tf, d), lambda i, j: (j, 0)),
            pl.BlockSpec((1, d), lambda i, j: (0, 0)),
        ],
        out_specs=pl.BlockSpec((tm, d), lambda i, j: (i, 0)),
        scratch_shapes=[pltpu.VMEM((tm, d), BF16)],
        compiler_params=pltpu.CompilerParams(
            dimension_semantics=("parallel", "arbitrary"),
            vmem_limit_bytes=VMEM_LIMIT_BYTES),
        name="ffn",
    )(x, g_pre, wg, wu, wd, g_post)


def _rope(xn, cos, sin_lo, sin_hi):
    return (xn * cos
            + pltpu.roll(xn, HEAD_DIM - ROPE_PAIRS, axis=1) * sin_lo
            + pltpu.roll(xn, ROPE_PAIRS, axis=1) * sin_hi)


def _in_proj_kernel(x_ref, g_ref, wq_ref, wk_ref, wv_ref, wx_ref, wy_ref, gq_ref, gk_ref,
                    cos_ref, slo_ref, shi_ref,
                    q_ref, kt_ref, v_ref, xr_ref, yr_ref):
    h = _rms(x_ref[0], g_ref[...]).astype(BF16)
    cos = cos_ref[...]
    slo = slo_ref[...]
    shi = shi_ref[...]

    q = jnp.dot(h, wq_ref[...], preferred_element_type=F32)
    gq = gq_ref[...] * (HEAD_DIM ** -0.5 * LOG2E)
    for hd in range(N_Q_HEADS):
        sl = slice(hd * HEAD_DIM, (hd + 1) * HEAD_DIM)
        q_ref[0, :, sl] = _rope(_rms(q[:, sl], gq), cos, slo, shi).astype(BF16)

    k = jnp.dot(h, wk_ref[...], preferred_element_type=F32)
    for hd in range(N_KV_HEADS):
        sl = slice(hd * HEAD_DIM, (hd + 1) * HEAD_DIM)
        kr = _rope(_rms(k[:, sl], gk_ref[...]), cos, slo, shi)
        kt_ref[0, hd] = kr.T.astype(BF16)

    v_ref[0] = jnp.dot(h, wv_ref[...], preferred_element_type=F32).astype(BF16)
    xr_ref[0] = jnp.dot(h, wx_ref[...], preferred_element_type=F32)
    yr_ref[0] = jnp.dot(h, wy_ref[...], preferred_element_type=F32)


def _in_proj(x, g, wq, wk, wv, wx, wy, gq, gk, cos, slo, shi, *, tm=512):
    b, s, d = x.shape
    tm = _tile(s, tm)
    tok = lambda w: pl.BlockSpec((1, tm, w), lambda i, j: (i, j, 0))
    tab = pl.BlockSpec((tm, HEAD_DIM), lambda i, j: (j, 0))
    return pl.pallas_call(
        _in_proj_kernel,
        out_shape=(
            jax.ShapeDtypeStruct((b, s, ATTN_WIDTH), BF16),
            jax.ShapeDtypeStruct((b, N_KV_HEADS, HEAD_DIM, s), BF16),
            jax.ShapeDtypeStruct((b, s, KV_WIDTH), BF16),
            jax.ShapeDtypeStruct((b, s, RNN_WIDTH), F32),
            jax.ShapeDtypeStruct((b, s, RNN_WIDTH), F32),
        ),
        grid=(b, s // tm),
        in_specs=[
            tok(d), _resident(g.shape),
            _resident(wq.shape), _resident(wk.shape), _resident(wv.shape),
            _resident(wx.shape), _resident(wy.shape),
            _resident(gq.shape), _resident(gk.shape),
            tab, tab, tab,
        ],
        out_specs=(
            tok(ATTN_WIDTH),
            pl.BlockSpec((1, N_KV_HEADS, HEAD_DIM, tm), lambda i, j: (i, 0, 0, j)),
            tok(KV_WIDTH), tok(RNN_WIDTH), tok(RNN_WIDTH),
        ),
        compiler_params=pltpu.CompilerParams(
            dimension_semantics=("parallel", "parallel"),
            vmem_limit_bytes=VMEM_LIMIT_BYTES),
        name="in_proj",
    )(x, g, wq, wk, wv, wx, wy, gq, gk, cos, slo, shi)


def _attn_kernel(q_ref, kt_ref, v_ref, qb_ref, o_ref, kmax_ref, *, tk_bounded, tk_online):
    tq = q_ref.shape[1]
    n_keys = kt_ref.shape[3]
    heads = [slice(hd * HEAD_DIM, (hd + 1) * HEAD_DIM) for hd in range(Q_PER_KV)]

    @pl.when(pl.program_id(2) == 0)
    def _():
        kf = kt_ref[0, 0].astype(F32)
        k2 = jnp.max(jnp.sum(kf * kf, axis=0, keepdims=True), axis=-1, keepdims=True)
        kmax_ref[...] = jnp.broadcast_to(jnp.sqrt(k2), kmax_ref.shape)

    m = qb_ref[:, :1] * kmax_ref[:, :1] * BOUND_SLACK
    bounded = 2.0 * jnp.max(m) <= SAFE_EXP2_RANGE

    def key_chunk(c, tk):
        start = pl.multiple_of(c * tk, tk)
        return kt_ref[0, 0, :, pl.ds(start, tk)], v_ref[0, pl.ds(start, tk), :]

    @pl.when(bounded)
    def _():
        def chunk(c, carry):
            kt, v = key_chunk(c, tk_bounded)
            out = []
            for sl, (l, acc) in zip(heads, carry):
                s = jnp.dot(q_ref[0, :, sl], kt, preferred_element_type=F32)
                p = jnp.exp2(s - m)
                l = l + jnp.sum(p, axis=-1, keepdims=True)
                acc = acc + jnp.dot(p.astype(BF16), v, preferred_element_type=F32)
                out.append((l, acc))
            return tuple(out)

        init = tuple((jnp.zeros((tq, 1), F32), jnp.zeros((tq, HEAD_DIM), F32)) for _ in heads)
        final = lax.fori_loop(0, n_keys // tk_bounded, chunk, init)
        for sl, (l, acc) in zip(heads, final):
            o_ref[0, :, sl] = acc / l

    @pl.when(jnp.logical_not(bounded))
    def _():
        def chunk(c, carry):
            kt, v = key_chunk(c, tk_online)
            out = []
            for sl, (m, l, acc) in zip(heads, carry):
                s = jnp.dot(q_ref[0, :, sl], kt, preferred_element_type=F32)
                m_new = jnp.maximum(m, jnp.max(s, axis=-1, keepdims=True))
                p = jnp.exp2(s - m_new)
                alpha = jnp.exp2(m - m_new)
                l = alpha * l + jnp.sum(p, axis=-1, keepdims=True)
                acc = alpha * acc + jnp.dot(p.astype(BF16), v, preferred_element_type=F32)
                out.append((m_new, l, acc))
            return tuple(out)

        init = tuple((jnp.full((tq, 1), -jnp.inf, F32), jnp.zeros((tq, 1), F32),
                      jnp.zeros((tq, HEAD_DIM), F32)) for _ in heads)
        final = lax.fori_loop(0, n_keys // tk_online, chunk, init)
        for sl, (_, l, acc) in zip(heads, final):
            o_ref[0, :, sl] = acc / l


def _attention(q, kt, v, q_bound, *, tq=1024, tk_bounded=4096, tk_online=2048):
    b, s, _ = q.shape
    tq = _tile(s, tq)
    tk_bounded = _tile(s, tk_bounded)
    tk_online = _tile(s, tk_online)
    gw = Q_PER_KV * HEAD_DIM
    return pl.pallas_call(
        functools.partial(_attn_kernel, tk_bounded=tk_bounded, tk_online=tk_online),
        out_shape=jax.ShapeDtypeStruct((b, s, ATTN_WIDTH), F32),
        grid=(b, N_KV_HEADS, s // tq),
        in_specs=[
            pl.BlockSpec((1, tq, gw), lambda i, g, j: (i, j, g)),
            pl.BlockSpec((1, 1, HEAD_DIM, s), lambda i, g, j: (i, g, 0, 0),
                         pipeline_mode=pl.Buffered(1)),
            pl.BlockSpec((1, s, HEAD_DIM), lambda i, g, j: (i, 0, g),
                         pipeline_mode=pl.Buffered(1)),
            _resident(q_bound.shape),
        ],
        out_specs=pl.BlockSpec((1, tq, gw), lambda i, g, j: (i, j, g)),
        scratch_shapes=[pltpu.VMEM((1, HEAD_DIM), F32)],
        compiler_params=pltpu.CompilerParams(
            dimension_semantics=("parallel", "parallel", "arbitrary"),
            vmem_limit_bytes=VMEM_LIMIT_BYTES),
        name="attention",
    )(q, kt, v, q_bound)


def _conv(row, x_ref, prev_ref, next_ref, wc_ref, bc_ref, ext_ref, xc_ref):
    t = pl.program_id(1)
    nt = pl.num_programs(1)
    tt = x_ref.shape[1]
    ext_ref[row, 0:SUBLANES] = jnp.where(t > 0, prev_ref[row], 0.0)
    ext_ref[row, SUBLANES:SUBLANES + tt] = x_ref[row]
    ext_ref[row, SUBLANES + tt:] = jnp.where(t < nt - 1, next_ref[row], 0.0)
    for n in range(RNN_BLOCKS):
        sl = slice(n * RNN_BLOCK_W, (n + 1) * RNN_BLOCK_W)
        ext = ext_ref[row, :, sl]
        xc = bc_ref[:, sl]
        for j in range(CONV_W):
            shift = (CONV_PAD_L - j) % ext.shape[0]
            tap = pltpu.roll(ext, shift, axis=0) if shift else ext
            xc = xc + tap[SUBLANES:SUBLANES + tt] * wc_ref[j:j + 1, sl]
        xc_ref[row, :, sl] = xc


def _gates(row, xc_ref, wg_ref, ba_ref, bx_ref, lam_ref, a_ref, u_ref):
    nl = -lam_ref[...]
    half_decay = (-0.5 * RG_C * LOG2E) * (jnp.maximum(nl, 0.0)
                                          + jnp.log(1.0 + jnp.exp(-jnp.abs(nl))))
    half_ba = 0.5 * ba_ref[...]
    half_bx = 0.5 * bx_ref[...]
    for n in range(RNN_BLOCKS):
        sl = slice(n * RNN_BLOCK_W, (n + 1) * RNN_BLOCK_W)
        xc = xc_ref[row, :, sl]
        gates = jnp.dot(xc.astype(BF16), wg_ref[n], preferred_element_type=F32)
        tr = jnp.tanh(gates[:, :RNN_BLOCK_W] + half_ba[:, sl])
        i = 0.5 * jnp.tanh(gates[:, RNN_BLOCK_W:] + half_bx[:, sl]) + 0.5
        a = jnp.exp2(half_decay[:, sl] * tr + half_decay[:, sl])
        a_ref[row, :, sl] = a
        v = 1.0 - a * a
        u_ref[row, :, sl] = jnp.where(v > 0.0, v * lax.rsqrt(v), 0.0) * (i * xc)


def _scan(a_ref, u_ref, o_ref, carry_ref, *, reverse):
    nb, tt, _ = a_ref.shape
    n_groups = tt // SCAN_GROUP

    def group(g, hs):
        first = (n_groups - 1 - g) * SCAN_GROUP if reverse else g * SCAN_GROUP
        rows = pl.ds(pl.multiple_of(first, SCAN_GROUP), SCAN_GROUP)
        views = [(a_ref.at[b, rows, :], u_ref.at[b, rows, :], o_ref.at[b, rows, :])
                 for b in range(nb)]
        hs = list(hs)
        for k in (reversed(range(SCAN_GROUP)) if reverse else range(SCAN_GROUP)):
            for b, (a_t, u_t, o_t) in enumerate(views):
                hs[b] = a_t[k:k + 1, :] * hs[b] + u_t[k:k + 1, :]
                o_t[k:k + 1, :] = hs[b]
        return tuple(hs)

    hs = lax.fori_loop(0, n_groups, group, tuple(carry_ref[b] for b in range(nb)))
    for b in range(nb):
        carry_ref[b] = hs[b]


def _rglru_fwd_kernel(x_ref, prev_ref, next_ref, wc_ref, bc_ref, wg_ref, ba_ref, bx_ref, lam_ref,
                      xc_ref, hf_ref, ext_ref, a_ref, u_ref, carry_ref):
    @pl.when(pl.program_id(1) == 0)
    def _():
        carry_ref[...] = jnp.zeros_like(carry_ref)

    for row in range(x_ref.shape[0]):
        _conv(row, x_ref, prev_ref, next_ref, wc_ref, bc_ref, ext_ref, xc_ref)
        _gates(row, xc_ref, wg_ref, ba_ref, bx_ref, lam_ref, a_ref, u_ref)
    _scan(a_ref, u_ref, hf_ref, carry_ref, reverse=False)


def _rglru_bwd_kernel(xc_ref, wg_ref, ba_ref, bx_ref, lam_ref, hf_ref, y_ref,
                      o_ref, a_ref, u_ref, hb_ref, carry_ref):
    @pl.when(pl.program_id(1) == 0)
    def _():
        carry_ref[...] = jnp.zeros_like(carry_ref)

    for row in range(xc_ref.shape[0]):
        _gates(row, xc_ref, wg_ref, ba_ref, bx_ref, lam_ref, a_ref, u_ref)
    _scan(a_ref, u_ref, hb_ref, carry_ref, reverse=True)

    for row in range(xc_ref.shape[0]):
        y = y_ref[row]
        gelu = 0.5 * y * (1.0 + jnp.tanh(math.sqrt(2.0 / math.pi) * (y + 0.044715 * (y * y * y))))
        o_ref[row] = (hf_ref[row] + hb_ref[row]) * gelu


def _rglru(xr, yr, wc, bc, wg_f, ba_f, bx_f, lam_f, wg_b, ba_b, bx_b, lam_b, *, tt=512):
    b, s, r = xr.shape
    tt = _tile(s, tt)
    nt = s // tt
    nb = _tile(b, RNN_BATCH_ROWS)
    groups = tt // SUBLANES
    last_group = s // SUBLANES - 1
    gate_params = [_resident(wg_f.shape), _resident(ba_f.shape), _resident(bx_f.shape),
                   _resident(lam_f.shape)]
    chunk = (nb, tt, r)
    cparams = pltpu.CompilerParams(dimension_semantics=("parallel", "arbitrary"),
                                   vmem_limit_bytes=VMEM_LIMIT_BYTES)
    full = jax.ShapeDtypeStruct((b, s, r), F32)

    cur = pl.BlockSpec(chunk, lambda i, j: (i, j, 0))
    prev = pl.BlockSpec((nb, SUBLANES, r), lambda i, j: (i, jnp.maximum(j * groups - 1, 0), 0))
    nxt = pl.BlockSpec((nb, SUBLANES, r),
                       lambda i, j: (i, jnp.minimum((j + 1) * groups, last_group), 0))
    xc, hf = pl.pallas_call(
        _rglru_fwd_kernel,
        out_shape=(full, full),
        grid=(b // nb, nt),
        in_specs=[cur, prev, nxt, _resident(wc.shape), _resident(bc.shape)] + gate_params,
        out_specs=(cur, cur),
        scratch_shapes=[pltpu.VMEM((nb, tt + 2 * SUBLANES, r), F32), pltpu.VMEM(chunk, F32),
                        pltpu.VMEM(chunk, F32), pltpu.VMEM((nb, 1, r), F32)],
        compiler_params=cparams,
        name="rglru_fwd",
    )(xr, xr, xr, wc, bc, wg_f, ba_f, bx_f, lam_f)

    rev = pl.BlockSpec(chunk, lambda i, j: (i, nt - 1 - j, 0))
    return pl.pallas_call(
        _rglru_bwd_kernel,
        out_shape=full,
        grid=(b // nb, nt),
        in_specs=[rev] + gate_params + [rev, rev],
        out_specs=rev,
        scratch_shapes=[pltpu.VMEM(chunk, F32), pltpu.VMEM(chunk, F32), pltpu.VMEM(chunk, F32),
                        pltpu.VMEM((nb, 1, r), F32)],
        compiler_params=cparams,
        name="rglru_bwd",
    )(xc, wg_b, ba_b, bx_b, lam_b, hf, yr)


def _out_proj_kernel(x_ref, a_ref, r_ref, ga_ref, gr_ref, wa_ref, wr_ref, gp_ref, o_ref):
    an = _rms(a_ref[...], ga_ref[...]).astype(BF16)
    rn = _rms(r_ref[...], gr_ref[...]).astype(BF16)
    y = (jnp.dot(an, wa_ref[...], preferred_element_type=F32)
         + jnp.dot(rn, wr_ref[...], preferred_element_type=F32))
    o_ref[...] = x_ref[...] + _rms(y, gp_ref[...])


def _out_proj(x, attn, rnn, ga, gr, wa, wr, gp, *, tm=512):
    t, d = x.shape
    tm = _tile(t, tm)
    tok = lambda w: pl.BlockSpec((tm, w), lambda i: (i, 0))
    return pl.pallas_call(
        _out_proj_kernel,
        out_shape=jax.ShapeDtypeStruct((t, d), F32),
        grid=(t // tm,),
        in_specs=[tok(d), tok(ATTN_WIDTH), tok(RNN_WIDTH),
                  _resident(ga.shape), _resident(gr.shape),
                  _resident(wa.shape), _resident(wr.shape), _resident(gp.shape)],
        out_specs=tok(d),
        compiler_params=pltpu.CompilerParams(
            dimension_semantics=("parallel",),
            vmem_limit_bytes=VMEM_LIMIT_BYTES),
        name="out_proj",
    )(x, attn, rnn, ga, gr, wa, wr, gp)


def _rope_tables(s):
    n_rows = s // GRID_W
    lane = jnp.arange(HEAD_DIM)
    inv_freq = ROPE_THETA ** (-jnp.arange(ROPE_PAIRS, dtype=F32) / ROPE_PAIRS)
    inv_lane = inv_freq[lane % ROPE_PAIRS]
    uses_col = lane >= 2 * ROPE_PAIRS
    upper = (lane % (2 * ROPE_PAIRS)) >= ROPE_PAIRS
    ang_r = (jnp.arange(n_rows, dtype=F32)[:, None] * inv_lane)[:, None, :]
    ang_c = (jnp.arange(GRID_W, dtype=F32)[:, None] * inv_lane)[None, :, :]
    cos = jnp.where(uses_col, jnp.cos(ang_c), jnp.cos(ang_r)).reshape(s, HEAD_DIM)
    sin = jnp.where(uses_col, jnp.sin(ang_c), jnp.sin(ang_r)).reshape(s, HEAD_DIM)
    sin_lo = jnp.where(upper, 0.0, -sin)
    sin_hi = jnp.where(upper, sin, 0.0)
    return cos, sin_lo, sin_hi


def _prepare(p):
    row = lambda v: v.reshape(1, -1)
    w_in = p["w_in"].astype(BF16)
    c0, c1, c2, c3 = ATTN_WIDTH, ATTN_WIDTH + KV_WIDTH, ATTN_WIDTH + 2 * KV_WIDTH, \
        ATTN_WIDTH + 2 * KV_WIDTH + RNN_WIDTH
    w_out = p["w_out"].astype(BF16)
    out = dict(
        ffn1=(row(p["g_ffn1_pre"]), p["w_ffn1_gate"].astype(BF16), p["w_ffn1_up"].astype(BF16),
              p["w_ffn1_down"].astype(BF16), row(p["g_ffn1_post"])),
        ffn2=(row(p["g_ffn2_pre"]), p["w_ffn2_gate"].astype(BF16), p["w_ffn2_up"].astype(BF16),
              p["w_ffn2_down"].astype(BF16), row(p["g_ffn2_post"])),
        in_proj=(row(p["g_mix_pre"]), w_in[:, :c0], w_in[:, c0:c1], w_in[:, c1:c2],
                 w_in[:, c2:c3], w_in[:, c3:], row(p["g_q"]), row(p["g_k"])),
        q_bound=jnp.full((1, HEAD_DIM), LOG2E, F32) * jnp.max(jnp.abs(p["g_q"])),
        rglru=(p["w_conv"], row(p["b_conv"]),
               (0.5 * jnp.concatenate([p["w_a_fwd"], p["w_x_fwd"]], axis=-1)).astype(BF16),
               row(p["b_a_fwd"]), row(p["b_x_fwd"]), row(p["lam_fwd"]),
               (0.5 * jnp.concatenate([p["w_a_bwd"], p["w_x_bwd"]], axis=-1)).astype(BF16),
               row(p["b_a_bwd"]), row(p["b_x_bwd"]), row(p["lam_bwd"])),
        out_proj=(row(p["g_attn_out"]), row(p["g_rnn_out"]), w_out[:ATTN_WIDTH], w_out[ATTN_WIDTH:],
                  row(p["g_mix_post"])),
    )
    return out


def _layer(x, w, rope):
    b, s, d = x.shape
    x1 = _ffn(x.reshape(b * s, d), *w["ffn1"])
    q, kt, v, xr, yr = _in_proj(x1.reshape(b, s, d), *w["in_proj"], *rope)
    attn = _attention(q, kt, v, w["q_bound"])
    rnn = _rglru(xr, yr, *w["rglru"])
    x2 = _out_proj(x1, attn.reshape(b * s, -1), rnn.reshape(b * s, -1), *w["out_proj"])
    x3 = _ffn(x2, *w["ffn2"])
    return x3.reshape(b, s, d)


_PARAM_NAMES = (
    "g_ffn1_pre", "w_ffn1_gate", "w_ffn1_up", "w_ffn1_down", "g_ffn1_post",
    "g_mix_pre", "w_in", "g_q", "g_k", "w_conv", "b_conv",
    "w_a_fwd", "b_a_fwd", "w_x_fwd", "b_x_fwd", "lam_fwd",
    "w_a_bwd", "b_a_bwd", "w_x_bwd", "b_x_bwd", "lam_bwd",
    "g_attn_out", "g_rnn_out", "w_out", "g_mix_post",
    "g_ffn2_pre", "w_ffn2_gate", "w_ffn2_up", "w_ffn2_down", "g_ffn2_post")


def kernel(x_prompt, x_sample, g_ffn1_pre, w_ffn1_gate, w_ffn1_up, w_ffn1_down, g_ffn1_post, g_mix_pre, w_in, g_q, g_k, w_conv, b_conv, w_a_fwd, b_a_fwd, w_x_fwd, b_x_fwd, lam_fwd, w_a_bwd, b_a_bwd, w_x_bwd, b_x_bwd, lam_bwd, g_attn_out, g_rnn_out, w_out, g_mix_post, g_ffn2_pre, w_ffn2_gate, w_ffn2_up, w_ffn2_down, g_ffn2_post):
    stacked = (g_ffn1_pre, w_ffn1_gate, w_ffn1_up, w_ffn1_down, g_ffn1_post, g_mix_pre, w_in, g_q,
               g_k, w_conv, b_conv, w_a_fwd, b_a_fwd, w_x_fwd, b_x_fwd, lam_fwd, w_a_bwd, b_a_bwd,
               w_x_bwd, b_x_bwd, lam_bwd, g_attn_out, g_rnn_out, w_out, g_mix_post, g_ffn2_pre,
               w_ffn2_gate, w_ffn2_up, w_ffn2_down, g_ffn2_post)
    depth = g_ffn1_pre.shape[0]
    y_prompt, y_sample = x_prompt, x_sample
    rope = _rope_tables(max(x_prompt.shape[1], x_sample.shape[1]))
    for layer in range(depth):
        w = _prepare({n: a[layer] for n, a in zip(_PARAM_NAMES, stacked)})
        y_prompt = _layer(y_prompt, w, rope)
        y_sample = _layer(y_sample, w, rope)
    return (y_prompt, y_sample)
```

```python
import functools
import math

import jax
import jax.numpy as jnp
from jax import lax
from jax.experimental import pallas as pl
from jax.experimental.pallas import tpu as pltpu

F32 = jnp.float32
BF16 = jnp.bfloat16

EPS = 1e-6
HEAD_DIM = 128
N_Q_HEADS = 8
N_KV_HEADS = 2
Q_PER_KV = N_Q_HEADS // N_KV_HEADS
ATTN_WIDTH = N_Q_HEADS * HEAD_DIM
KV_WIDTH = N_KV_HEADS * HEAD_DIM
RNN_BLOCKS = 8
RNN_BLOCK_W = 128
RNN_WIDTH = RNN_BLOCKS * RNN_BLOCK_W
GRID_W = 64
ROPE_PAIRS = HEAD_DIM // 4
ROPE_THETA = 10000.0
CONV_W = 4
CONV_PAD_L = 2
RG_C = 8.0
LOG2E = math.log2(math.e)
SUBLANES = 8
SCAN_GROUP = 16
RNN_BATCH_ROWS = 2
NORM_ROWS = 32
SAFE_EXP2_RANGE = 100.0
BOUND_SLACK = 1.0 + 2.0 ** -6

VMEM_LIMIT_BYTES = 58 * 1024 * 1024


def _tile(n, want):
    t = min(n, want)
    assert n % t == 0, (n, want)
    return t


def _rms(x, g):
    ms = jnp.mean(x * x, axis=-1, keepdims=True)
    return x * lax.rsqrt(ms + EPS) * g


def _silu(x):
    h = 0.5 * x
    return h * jnp.tanh(h) + h


def _resident(shape):
    nd = len(shape)
    return pl.BlockSpec(shape, lambda *_: (0,) * nd, pipeline_mode=pl.Buffered(1))


def _ffn_kernel(x_ref, gpre_ref, wg_ref, wu_ref, wd_ref, gpost_ref, o_ref, h_ref, *, row_split):
    j = pl.program_id(1)
    last = pl.num_programs(1) - 1
    tm, _ = x_ref.shape
    hm = tm // row_split

    def step(is_first, is_last):
        gp = 0.5 * gpost_ref[...]
        for r in range(row_split):
            rows = slice(r * hm, (r + 1) * hm)
            chunks = [slice(r * hm + i * NORM_ROWS, r * hm + (i + 1) * NORM_ROWS)
                      for i in range(hm // NORM_ROWS)]
            if is_first:
                for c in chunks:
                    h_ref[c, :] = _rms(x_ref[c, :], gpre_ref[...]).astype(BF16)
            h = h_ref[rows, :]
            g = jnp.dot(h, wg_ref[...], preferred_element_type=F32)
            u = jnp.dot(h, wu_ref[...], preferred_element_type=F32)
            a = (_silu(g) * u).astype(BF16)
            down = jnp.dot(a, wd_ref[...], preferred_element_type=F32)
            if is_first:
                o_ref[rows, :] = down
            else:
                o_ref[rows, :] += down
            if is_last:
                for c in chunks:
                    o_ref[c, :] = x_ref[c, :] + _rms(o_ref[c, :], gp)

    pl.when(j == 0)(lambda: step(True, False))
    pl.when(jnp.logical_and(j > 0, j < last))(lambda: step(False, False))
    pl.when(j == last)(lambda: step(False, True))


def _ffn(x, g_pre, wg, wu, wd, g_post, *, tm=1024, tf=512):
    t, d = x.shape
    f = wg.shape[1]
    tm = _tile(t, tm)
    tf = _tile(f, tf)
    assert f // tf >= 2, "first and last d_ff steps must be distinct grid steps"
    row_split = 2 if tm % (2 * NORM_ROWS) == 0 else 1
    return pl.pallas_call(
        functools.partial(_ffn_kernel, row_split=row_split),
        out_shape=jax.ShapeDtypeStruct((t, d), F32),
        grid=(t // tm, f // tf),
        in_specs=[
            pl.BlockSpec((tm, d), lambda i, j: (i, 0)),
            pl.BlockSpec((1, d), lambda i, j: (0, 0)),
            pl.BlockSpec((d, tf), lambda i, j: (0, j)),
            pl.BlockSpec((d, tf), lambda i, j: (0, j)),
            pl.BlockSpec((tf, d), lambda i, j: (j, 0)),
            pl.BlockSpec((1, d), lambda i, j: (0, 0)),
        ],
        out_specs=pl.BlockSpec((tm, d), lambda i, j: (i, 0)),
        scratch_shapes=[pltpu.VMEM((tm, d), BF16)],
        compiler_params=pltpu.CompilerParams(
            dimension_semantics=("parallel", "arbitrary"),
            vmem_limit_bytes=VMEM_LIMIT_BYTES),
        name="ffn",
    )(x, g_pre, wg, wu, wd, g_post)


def _rope(xn, cos, sin_lo, sin_hi):
    return (xn * cos
            + pltpu.roll(xn, HEAD_DIM - ROPE_PAIRS, axis=1) * sin_lo
            + pltpu.roll(xn, ROPE_PAIRS, axis=1) * sin_hi)


def _in_proj_kernel(x_ref, g_ref, wq_ref, wk_ref, wv_ref, wx_ref, wy_ref, gq_ref, gk_ref,
                    cos_ref, slo_ref, shi_ref,
                    q_ref, kt_ref, v_ref, xr_ref, yr_ref):
    h = _rms(x_ref[0], g_ref[...]).astype(BF16)
    cos = cos_ref[...]
    slo = slo_ref[...]
    shi = shi_ref[...]

    q = jnp.dot(h, wq_ref[...], preferred_element_type=F32)
    gq = gq_ref[...] * (HEAD_DIM ** -0.5 * LOG2E)
    for hd in range(N_Q_HEADS):
        sl = slice(hd * HEAD_DIM, (hd + 1) * HEAD_DIM)
        q_ref[0, :, sl] = _rope(_rms(q[:, sl], gq), cos, slo, shi).astype(BF16)

    k = jnp.dot(h, wk_ref[...], preferred_element_type=F32)
    for hd in range(N_KV_HEADS):
        sl = slice(hd * HEAD_DIM, (hd + 1) * HEAD_DIM)
        kr = _rope(_rms(k[:, sl], gk_ref[...]), cos, slo, shi)
        kt_ref[0, hd] = kr.T.astype(BF16)

    v_ref[0] = jnp.dot(h, wv_ref[...], preferred_element_type=F32).astype(BF16)
    xr_ref[0] = jnp.dot(h, wx_ref[...], preferred_element_type=F32)
    yr_ref[0] = jnp.dot(h, wy_ref[...], preferred_element_type=F32)


def _in_proj(x, g, wq, wk, wv, wx, wy, gq, gk, cos, slo, shi, *, tm=512):
    b, s, d = x.shape
    tm = _tile(s, tm)
    tok = lambda w: pl.BlockSpec((1, tm, w), lambda i, j: (i, j, 0))
    tab = pl.BlockSpec((tm, HEAD_DIM), lambda i, j: (j, 0))
    return pl.pallas_call(
        _in_proj_kernel,
        out_shape=(
            jax.ShapeDtypeStruct((b, s, ATTN_WIDTH), BF16),
            jax.ShapeDtypeStruct((b, N_KV_HEADS, HEAD_DIM, s), BF16),
            jax.ShapeDtypeStruct((b, s, KV_WIDTH), BF16),
            jax.ShapeDtypeStruct((b, s, RNN_WIDTH), F32),
            jax.ShapeDtypeStruct((b, s, RNN_WIDTH), F32),
        ),
        grid=(b, s // tm),
        in_specs=[
            tok(d), _resident(g.shape),
            _resident(wq.shape), _resident(wk.shape), _resident(wv.shape),
            _resident(wx.shape), _resident(wy.shape),
            _resident(gq.shape), _resident(gk.shape),
            tab, tab, tab,
        ],
        out_specs=(
            tok(ATTN_WIDTH),
            pl.BlockSpec((1, N_KV_HEADS, HEAD_DIM, tm), lambda i, j: (i, 0, 0, j)),
            tok(KV_WIDTH), tok(RNN_WIDTH), tok(RNN_WIDTH),
        ),
        compiler_params=pltpu.CompilerParams(
            dimension_semantics=("parallel", "parallel"),
            vmem_limit_bytes=VMEM_LIMIT_BYTES),
        name="in_proj",
    )(x, g, wq, wk, wv, wx, wy, gq, gk, cos, slo, shi)


def _attn_kernel(q_ref, kt_ref, v_ref, qb_ref, o_ref, kmax_ref, *, tk_bounded, tk_online):
    tq = q_ref.shape[1]
    n_keys = kt_ref.shape[3]
    heads = [slice(hd * HEAD_DIM, (hd + 1) * HEAD_DIM) for hd in range(Q_PER_KV)]

    @pl.when(pl.program_id(2) == 0)
    def _():
        kf = kt_ref[0, 0].astype(F32)
        k2 = jnp.max(jnp.sum(kf * kf, axis=0, keepdims=True), axis=-1, keepdims=True)
        kmax_ref[...] = jnp.broadcast_to(jnp.sqrt(k2), kmax_ref.shape)

    m = qb_ref[:, :1] * kmax_ref[:, :1] * BOUND_SLACK
    bounded = 2.0 * jnp.max(m) <= SAFE_EXP2_RANGE

    def key_chunk(c, tk):
        start = pl.multiple_of(c * tk, tk)
        return kt_ref[0, 0, :, pl.ds(start, tk)], v_ref[0, pl.ds(start, tk), :]

    @pl.when(bounded)
    def _():
        def chunk(c, carry):
            kt, v = key_chunk(c, tk_bounded)
            out = []
            for sl, (l, acc) in zip(heads, carry):
                s = jnp.dot(q_ref[0, :, sl], kt, preferred_element_type=F32)
                p = jnp.exp2(s)
                l = l + jnp.sum(p, axis=-1, keepdims=True)
                acc = acc + jnp.dot(p.astype(BF16), v, preferred_element_type=F32)
                out.append((l, acc))
            return tuple(out)

        init = tuple((jnp.zeros((tq, 1), F32), jnp.zeros((tq, HEAD_DIM), F32)) for _ in heads)
        final = lax.fori_loop(0, n_keys // tk_bounded, chunk, init)
        for sl, (l, acc) in zip(heads, final):
            o_ref[0, :, sl] = acc / l

    @pl.when(jnp.logical_not(bounded))
    def _():
        def chunk(c, carry):
            kt, v = key_chunk(c, tk_online)
            out = []
            for sl, (m, l, acc) in zip(heads, carry):
                s = jnp.dot(q_ref[0, :, sl], kt, preferred_element_type=F32)
                m_new = jnp.maximum(m, jnp.max(s, axis=-1, keepdims=True))
                p = jnp.exp2(s - m_new)
                alpha = jnp.exp2(m - m_new)
                l = alpha * l + jnp.sum(p, axis=-1, keepdims=True)
                acc = alpha * acc + jnp.dot(p.astype(BF16), v, preferred_element_type=F32)
                out.append((m_new, l, acc))
            return tuple(out)

        init = tuple((jnp.full((tq, 1), -jnp.inf, F32), jnp.zeros((tq, 1), F32),
                      jnp.zeros((tq, HEAD_DIM), F32)) for _ in heads)
        final = lax.fori_loop(0, n_keys // tk_online, chunk, init)
        for sl, (_, l, acc) in zip(heads, final):
            o_ref[0, :, sl] = acc / l


def _attention(q, kt, v, q_bound, *, tq=1024, tk_bounded=4096, tk_online=2048):
    b, s, _ = q.shape
    tq = _tile(s, tq)
    tk_bounded = _tile(s, tk_bounded)
    tk_online = _tile(s, tk_online)
    gw = Q_PER_KV * HEAD_DIM
    return pl.pallas_call(
        functools.partial(_attn_kernel, tk_bounded=tk_bounded, tk_online=tk_online),
        out_shape=jax.ShapeDtypeStruct((b, s, ATTN_WIDTH), F32),
        grid=(b, N_KV_HEADS, s // tq),
        in_specs=[
            pl.BlockSpec((1, tq, gw), lambda i, g, j: (i, j, g)),
            pl.BlockSpec((1, 1, HEAD_DIM, s), lambda i, g, j: (i, g, 0, 0),
                         pipeline_mode=pl.Buffered(1)),
            pl.BlockSpec((1, s, HEAD_DIM), lambda i, g, j: (i, 0, g),
                         pipeline_mode=pl.Buffered(1)),
            _resident(q_bound.shape),
        ],
        out_specs=pl.BlockSpec((1, tq, gw), lambda i, g, j: (i, j, g)),
        scratch_shapes=[pltpu.VMEM((1, HEAD_DIM), F32)],
        compiler_params=pltpu.CompilerParams(
            dimension_semantics=("parallel", "parallel", "arbitrary"),
            vmem_limit_bytes=VMEM_LIMIT_BYTES),
        name="attention",
    )(q, kt, v, q_bound)


def _conv(row, x_ref, prev_ref, next_ref, wc_ref, bc_ref, ext_ref, xc_ref):
    t = pl.program_id(1)
    nt = pl.num_programs(1)
    tt = x_ref.shape[1]
    ext_ref[row, 0:SUBLANES] = jnp.where(t > 0, prev_ref[row], 0.0)
    ext_ref[row, SUBLANES:SUBLANES + tt] = x_ref[row]
    ext_ref[row, SUBLANES + tt:] = jnp.where(t < nt - 1, next_ref[row], 0.0)
    for n in range(RNN_BLOCKS):
        sl = slice(n * RNN_BLOCK_W, (n + 1) * RNN_BLOCK_W)
        ext = ext_ref[row, :, sl]
        xc = bc_ref[:, sl]
        for j in range(CONV_W):
            shift = (CONV_PAD_L - j) % ext.shape[0]
            tap = pltpu.roll(ext, shift, axis=0) if shift else ext
            xc = xc + tap[SUBLANES:SUBLANES + tt] * wc_ref[j:j + 1, sl]
        xc_ref[row, :, sl] = xc


def _gates(row, xc_ref, wg_ref, ba_ref, bx_ref, lam_ref, a_ref, u_ref):
    nl = -lam_ref[...]
    half_decay = (-0.5 * RG_C * LOG2E) * (jnp.maximum(nl, 0.0)
                                          + jnp.log(1.0 + jnp.exp(-jnp.abs(nl))))
    half_ba = 0.5 * ba_ref[...]
    half_bx = 0.5 * bx_ref[...]
    for n in range(RNN_BLOCKS):
        sl = slice(n * RNN_BLOCK_W, (n + 1) * RNN_BLOCK_W)
        xc = xc_ref[row, :, sl]
        gates = jnp.dot(xc.astype(BF16), wg_ref[n], preferred_element_type=F32)
        tr = jnp.tanh(gates[:, :RNN_BLOCK_W] + half_ba[:, sl])
        i = 0.5 * jnp.tanh(gates[:, RNN_BLOCK_W:] + half_bx[:, sl]) + 0.5
        a = jnp.exp2(half_decay[:, sl] * tr + half_decay[:, sl])
        a_ref[row, :, sl] = a
        v = 1.0 - a * a
        u_ref[row, :, sl] = jnp.where(v > 0.0, v * lax.rsqrt(v), 0.0) * (i * xc)


def _scan(a_ref, u_ref, o_ref, carry_ref, *, reverse):
    nb, tt, _ = a_ref.shape
    n_groups = tt // SCAN_GROUP

    def group(g, hs):
        first = (n_groups - 1 - g) * SCAN_GROUP if reverse else g * SCAN_GROUP
        rows = pl.ds(pl.multiple_of(first, SCAN_GROUP), SCAN_GROUP)
        views = [(a_ref.at[b, rows, :], u_ref.at[b, rows, :], o_ref.at[b, rows, :])
                 for b in range(nb)]
        hs = list(hs)
        for k in (reversed(range(SCAN_GROUP)) if reverse else range(SCAN_GROUP)):
            for b, (a_t, u_t, o_t) in enumerate(views):
                hs[b] = a_t[k:k + 1, :] * hs[b] + u_t[k:k + 1, :]
                o_t[k:k + 1, :] = hs[b]
        return tuple(hs)

    hs = lax.fori_loop(0, n_groups, group, tuple(carry_ref[b] for b in range(nb)))
    for b in range(nb):
        carry_ref[b] = hs[b]


def _rglru_fwd_kernel(x_ref, prev_ref, next_ref, wc_ref, bc_ref, wg_ref, ba_ref, bx_ref, lam_ref,
                      xc_ref, hf_ref, ext_ref, a_ref, u_ref, carry_ref):
    @pl.when(pl.program_id(1) == 0)
    def _():
        carry_ref[...] = jnp.zeros_like(carry_ref)

    for row in range(x_ref.shape[0]):
        _conv(row, x_ref, prev_ref, next_ref, wc_ref, bc_ref, ext_ref, xc_ref)
        _gates(row, xc_ref, wg_ref, ba_ref, bx_ref, lam_ref, a_ref, u_ref)
    _scan(a_ref, u_ref, hf_ref, carry_ref, reverse=False)


def _rglru_bwd_kernel(xc_ref, wg_ref, ba_ref, bx_ref, lam_ref, hf_ref, y_ref,
                      o_ref, a_ref, u_ref, hb_ref, carry_ref):
    @pl.when(pl.program_id(1) == 0)
    def _():
        carry_ref[...] = jnp.zeros_like(carry_ref)

    for row in range(xc_ref.shape[0]):
        _gates(row, xc_ref, wg_ref, ba_ref, bx_ref, lam_ref, a_ref, u_ref)
    _scan(a_ref, u_ref, hb_ref, carry_ref, reverse=True)

    for row in range(xc_ref.shape[0]):
        y = y_ref[row]
        gelu = 0.5 * y * (1.0 + jnp.tanh(math.sqrt(2.0 / math.pi) * (y + 0.044715 * (y * y * y))))
        o_ref[row] = (hf_ref[row] + hb_ref[row]) * gelu


def _rglru(xr, yr, wc, bc, wg_f, ba_f, bx_f, lam_f, wg_b, ba_b, bx_b, lam_b, *, tt=512):
    b, s, r = xr.shape
    tt = _tile(s, tt)
    nt = s // tt
    nb = _tile(b, RNN_BATCH_ROWS)
    groups = tt // SUBLANES
    last_group = s // SUBLANES - 1
    gate_params = [_resident(wg_f.shape), _resident(ba_f.shape), _resident(bx_f.shape),
                   _resident(lam_f.shape)]
    chunk = (nb, tt, r)
    cparams = pltpu.CompilerParams(dimension_semantics=("parallel", "arbitrary"),
                                   vmem_limit_bytes=VMEM_LIMIT_BYTES)
    full = jax.ShapeDtypeStruct((b, s, r), F32)

    cur = pl.BlockSpec(chunk, lambda i, j: (i, j, 0))
    prev = pl.BlockSpec((nb, SUBLANES, r), lambda i, j: (i, jnp.maximum(j * groups - 1, 0), 0))
    nxt = pl.BlockSpec((nb, SUBLANES, r),
                       lambda i, j: (i, jnp.minimum((j + 1) * groups, last_group), 0))
    xc, hf = pl.pallas_call(
        _rglru_fwd_kernel,
        out_shape=(full, full),
        grid=(b // nb, nt),
        in_specs=[cur, prev, nxt, _resident(wc.shape), _resident(bc.shape)] + gate_params,
        out_specs=(cur, cur),
        scratch_shapes=[pltpu.VMEM((nb, tt + 2 * SUBLANES, r), F32), pltpu.VMEM(chunk, F32),
                        pltpu.VMEM(chunk, F32), pltpu.VMEM((nb, 1, r), F32)],
        compiler_params=cparams,
        name="rglru_fwd",
    )(xr, xr, xr, wc, bc, wg_f, ba_f, bx_f, lam_f)

    rev = pl.BlockSpec(chunk, lambda i, j: (i, nt - 1 - j, 0))
    return pl.pallas_call(
        _rglru_bwd_kernel,
        out_shape=full,
        grid=(b // nb, nt),
        in_specs=[rev] + gate_params + [rev, rev],
        out_specs=rev,
        scratch_shapes=[pltpu.VMEM(chunk, F32), pltpu.VMEM(chunk, F32), pltpu.VMEM(chunk, F32),
                        pltpu.VMEM((nb, 1, r), F32)],
        compiler_params=cparams,
        name="rglru_bwd",
    )(xc, wg_b, ba_b, bx_b, lam_b, hf, yr)


def _out_proj_kernel(x_ref, a_ref, r_ref, ga_ref, gr_ref, wa_ref, wr_ref, gp_ref, o_ref):
    an = _rms(a_ref[...], ga_ref[...]).astype(BF16)
    rn = _rms(r_ref[...], gr_ref[...]).astype(BF16)
    y = (jnp.dot(an, wa_ref[...], preferred_element_type=F32)
         + jnp.dot(rn, wr_ref[...], preferred_element_type=F32))
    o_ref[...] = x_ref[...] + _rms(y, gp_ref[...])


def _out_proj(x, attn, rnn, ga, gr, wa, wr, gp, *, tm=512):
    t, d = x.shape
    tm = _tile(t, tm)
    tok = lambda w: pl.BlockSpec((tm, w), lambda i: (i, 0))
    return pl.pallas_call(
        _out_proj_kernel,
        out_shape=jax.ShapeDtypeStruct((t, d), F32),
        grid=(t // tm,),
        in_specs=[tok(d), tok(ATTN_WIDTH), tok(RNN_WIDTH),
                  _resident(ga.shape), _resident(gr.shape),
                  _resident(wa.shape), _resident(wr.shape), _resident(gp.shape)],
        out_specs=tok(d),
        compiler_params=pltpu.CompilerParams(
            dimension_semantics=("parallel",),
            vmem_limit_bytes=VMEM_LIMIT_BYTES),
        name="out_proj",
    )(x, attn, rnn, ga, gr, wa, wr, gp)


def _rope_tables(s):
    n_rows = s // GRID_W
    lane = jnp.arange(HEAD_DIM)
    inv_freq = ROPE_THETA ** (-jnp.arange(ROPE_PAIRS, dtype=F32) / ROPE_PAIRS)
    inv_lane = inv_freq[lane % ROPE_PAIRS]
    uses_col = lane >= 2 * ROPE_PAIRS
    upper = (lane % (2 * ROPE_PAIRS)) >= ROPE_PAIRS
    ang_r = (jnp.arange(n_rows, dtype=F32)[:, None] * inv_lane)[:, None, :]
    ang_c = (jnp.arange(GRID_W, dtype=F32)[:, None] * inv_lane)[None, :, :]
    cos = jnp.where(uses_col, jnp.cos(ang_c), jnp.cos(ang_r)).reshape(s, HEAD_DIM)
    sin = jnp.where(uses_col, jnp.sin(ang_c), jnp.sin(ang_r)).reshape(s, HEAD_DIM)
    sin_lo = jnp.where(upper, 0.0, -sin)
    sin_hi = jnp.where(upper, sin, 0.0)
    return cos, sin_lo, sin_hi


def _prepare(p):
    row = lambda v: v.reshape(1, -1)
    w_in = p["w_in"].astype(BF16)
    c0, c1, c2, c3 = ATTN_WIDTH, ATTN_WIDTH + KV_WIDTH, ATTN_WIDTH + 2 * KV_WIDTH, \
        ATTN_WIDTH + 2 * KV_WIDTH + RNN_WIDTH
    w_out = p["w_out"].astype(BF16)
    out = dict(
        ffn1=(row(p["g_ffn1_pre"]), p["w_ffn1_gate"].astype(BF16), p["w_ffn1_up"].astype(BF16),
              p["w_ffn1_down"].astype(BF16), row(p["g_ffn1_post"])),
        ffn2=(row(p["g_ffn2_pre"]), p["w_ffn2_gate"].astype(BF16), p["w_ffn2_up"].astype(BF16),
              p["w_ffn2_down"].astype(BF16), row(p["g_ffn2_post"])),
        in_proj=(row(p["g_mix_pre"]), w_in[:, :c0], w_in[:, c0:c1], w_in[:, c1:c2],
                 w_in[:, c2:c3], w_in[:, c3:], row(p["g_q"]), row(p["g_k"])),
        q_bound=jnp.full((1, HEAD_DIM), LOG2E, F32) * jnp.max(jnp.abs(p["g_q"])),
        rglru=(p["w_conv"], row(p["b_conv"]),
               (0.5 * jnp.concatenate([p["w_a_fwd"], p["w_x_fwd"]], axis=-1)).astype(BF16),
               row(p["b_a_fwd"]), row(p["b_x_fwd"]), row(p["lam_fwd"]),
               (0.5 * jnp.concatenate([p["w_a_bwd"], p["w_x_bwd"]], axis=-1)).astype(BF16),
               row(p["b_a_bwd"]), row(p["b_x_bwd"]), row(p["lam_bwd"])),
        out_proj=(row(p["g_attn_out"]), row(p["g_rnn_out"]), w_out[:ATTN_WIDTH], w_out[ATTN_WIDTH:],
                  row(p["g_mix_post"])),
    )
    return out


def _layer(x, w, rope):
    b, s, d = x.shape
    x1 = _ffn(x.reshape(b * s, d), *w["ffn1"])
    q, kt, v, xr, yr = _in_proj(x1.reshape(b, s, d), *w["in_proj"], *rope)
    attn = _attention(q, kt, v, w["q_bound"])
    rnn = _rglru(xr, yr, *w["rglru"])
    x2 = _out_proj(x1, attn.reshape(b * s, -1), rnn.reshape(b * s, -1), *w["out_proj"])
    x3 = _ffn(x2, *w["ffn2"])
    return x3.reshape(b, s, d)


_PARAM_NAMES = (
    "g_ffn1_pre", "w_ffn1_gate", "w_ffn1_up", "w_ffn1_down", "g_ffn1_post",
    "g_mix_pre", "w_in", "g_q", "g_k", "w_conv", "b_conv",
    "w_a_fwd", "b_a_fwd", "w_x_fwd", "b_x_fwd", "lam_fwd",
    "w_a_bwd", "b_a_bwd", "w_x_bwd", "b_x_bwd", "lam_bwd",
    "g_attn_out", "g_rnn_out", "w_out", "g_mix_post",
    "g_ffn2_pre", "w_ffn2_gate", "w_ffn2_up", "w_ffn2_down", "g_ffn2_post")


def kernel(x_prompt, x_sample, g_ffn1_pre, w_ffn1_gate, w_ffn1_up, w_ffn1_down, g_ffn1_post, g_mix_pre, w_in, g_q, g_k, w_conv, b_conv, w_a_fwd, b_a_fwd, w_x_fwd, b_x_fwd, lam_fwd, w_a_bwd, b_a_bwd, w_x_bwd, b_x_bwd, lam_bwd, g_attn_out, g_rnn_out, w_out, g_mix_post, g_ffn2_pre, w_ffn2_gate, w_ffn2_up, w_ffn2_down, g_ffn2_post):
    stacked = (g_ffn1_pre, w_ffn1_gate, w_ffn1_up, w_ffn1_down, g_ffn1_post, g_mix_pre, w_in, g_q,
               g_k, w_conv, b_conv, w_a_fwd, b_a_fwd, w_x_fwd, b_x_fwd, lam_fwd, w_a_bwd, b_a_bwd,
               w_x_bwd, b_x_bwd, lam_bwd, g_attn_out, g_rnn_out, w_out, g_mix_post, g_ffn2_pre,
               w_ffn2_gate, w_ffn2_up, w_ffn2_down, g_ffn2_post)
    depth = g_ffn1_pre.shape[0]
    y_prompt, y_sample = x_prompt, x_sample
    rope = _rope_tables(max(x_prompt.shape[1], x_sample.shape[1]))
    for layer in range(depth):
        w = _prepare({n: a[layer] for n, a in zip(_PARAM_NAMES, stacked)})
        y_prompt = _layer(y_prompt, w, rope)
        y_sample = _layer(y_sample, w, rope)
    return (y_prompt, y_sample)
```

```python
import functools
import math

import jax
import jax.numpy as jnp
from jax import lax
from jax.experimental import pallas as pl
from jax.experimental.pallas import tpu as pltpu

F32 = jnp.float32
BF16 = jnp.bfloat16

EPS = 1e-6
HEAD_DIM = 128
N_Q_HEADS = 8
N_KV_HEADS = 2
Q_PER_KV = N_Q_HEADS // N_KV_HEADS
ATTN_WIDTH = N_Q_HEADS * HEAD_DIM
KV_WIDTH = N_KV_HEADS * HEAD_DIM
RNN_BLOCKS = 8
RNN_BLOCK_W = 128
RNN_WIDTH = RNN_BLOCKS * RNN_BLOCK_W
GRID_W = 64
ROPE_PAIRS = HEAD_DIM // 4
ROPE_THETA = 10000.0
CONV_W = 4
CONV_PAD_L = 2
RG_C = 8.0
LOG2E = math.log2(math.e)
SUBLANES = 8
SCAN_GROUP = 16
RNN_BATCH_ROWS = 2
NORM_ROWS = 32
SAFE_EXP2_RANGE = 100.0
BOUND_SLACK = 1.0 + 2.0 ** -6

VMEM_LIMIT_BYTES = 58 * 1024 * 1024


def _tile(n, want):
    t = min(n, want)
    assert n % t == 0, (n, want)
    return t


def _rms(x, g):
    ms = jnp.mean(x * x, axis=-1, keepdims=True)
    return x * lax.rsqrt(ms + EPS) * g


def _silu(x):
    h = 0.5 * x
    return h * jnp.tanh(h) + h


def _resident(shape):
    nd = len(shape)
    return pl.BlockSpec(shape, lambda *_: (0,) * nd, pipeline_mode=pl.Buffered(1))


def _ffn_kernel(x_ref, gpre_ref, wg_ref, wu_ref, wd_ref, gpost_ref, o_ref, h_ref, *, row_split):
    j = pl.program_id(1)
    last = pl.num_programs(1) - 1
    tm, _ = x_ref.shape
    hm = tm // row_split

    def step(is_first, is_last):
        gp = 0.5 * gpost_ref[...]
        for r in range(row_split):
            rows = slice(r * hm, (r + 1) * hm)
            chunks = [slice(r * hm + i * NORM_ROWS, r * hm + (i + 1) * NORM_ROWS)
                      for i in range(hm // NORM_ROWS)]
            if is_first:
                for c in chunks:
                    h_ref[c, :] = _rms(x_ref[c, :], gpre_ref[...]).astype(BF16)
            h = h_ref[rows, :]
            g = jnp.dot(h, wg_ref[...], preferred_element_type=F32)
            u = jnp.dot(h, wu_ref[...], preferred_element_type=F32)
            a = (_silu(g) * u).astype(BF16)
            down = jnp.dot(a, wd_ref[...], preferred_element_type=F32)
            if is_first:
                o_ref[rows, :] = down
            else:
                o_ref[rows, :] += down
            if is_last:
                for c in chunks:
                    o_ref[c, :] = x_ref[c, :] + _rms(o_ref[c, :], gp)

    pl.when(j == 0)(lambda: step(True, False))
    pl.when(jnp.logical_and(j > 0, j < last))(lambda: step(False, False))
    pl.when(j == last)(lambda: step(False, True))


def _ffn(x, g_pre, wg, wu, wd, g_post, *, tm=1024, tf=512):
    t, d = x.shape
    f = wg.shape[1]
    tm = _tile(t, tm)
    tf = _tile(f, tf)
    assert f // tf >= 2, "first and last d_ff steps must be distinct grid steps"
    row_split = 2 if tm % (2 * NORM_ROWS) == 0 else 1
    return pl.pallas_call(
        functools.partial(_ffn_kernel, row_split=row_split),
        out_shape=jax.ShapeDtypeStruct((t, d), F32),
        grid=(t // tm, f // tf),
        in_specs=[
            pl.BlockSpec((tm, d), lambda i, j: (i, 0)),
            pl.BlockSpec((1, d), lambda i, j: (0, 0)),
            pl.BlockSpec((d, tf), lambda i, j: (0, j)),
            pl.BlockSpec((d, tf), lambda i, j: (0, j)),
            pl.BlockSpec((tf, d), lambda i, j: (j, 0)),
            pl.BlockSpec((1, d), lambda i, j: (0, 0)),
        ],
        out_specs=pl.BlockSpec((tm, d), lambda i, j: (i, 0)),
        scratch_shapes=[pltpu.VMEM((tm, d), BF16)],
        compiler_params=pltpu.CompilerParams(
            dimension_semantics=("parallel", "arbitrary"),
            vmem_limit_bytes=VMEM_LIMIT_BYTES),
        name="ffn",
    )(x, g_pre, wg, wu, wd, g_post)


def _rope(xn, cos, sin_lo, sin_hi):
    return (xn * cos
            + pltpu.roll(xn, HEAD_DIM - ROPE_PAIRS, axis=1) * sin_lo
            + pltpu.roll(xn, ROPE_PAIRS, axis=1) * sin_hi)


def _in_proj_kernel(x_ref, g_ref, wq_ref, wk_ref, wv_ref, wx_ref, wy_ref, gq_ref, gk_ref,
                    cos_ref, slo_ref, shi_ref,
                    q_ref, kt_ref, v_ref, xr_ref, yr_ref):
    h = _rms(x_ref[0], g_ref[...]).astype(BF16)
    cos = cos_ref[...]
    slo = slo_ref[...]
    shi = shi_ref[...]

    q = jnp.dot(h, wq_ref[...], preferred_element_type=F32)
    gq = gq_ref[...] * (HEAD_DIM ** -0.5 * LOG2E)
    for hd in range(N_Q_HEADS):
        sl = slice(hd * HEAD_DIM, (hd + 1) * HEAD_DIM)
        q_ref[0, :, sl] = _rope(_rms(q[:, sl], gq), cos, slo, shi).astype(BF16)

    k = jnp.dot(h, wk_ref[...], preferred_element_type=F32)
    for hd in range(N_KV_HEADS):
        sl = slice(hd * HEAD_DIM, (hd + 1) * HEAD_DIM)
        kr = _rope(_rms(k[:, sl], gk_ref[...]), cos, slo, shi)
        kt_ref[0, hd] = kr.T.astype(BF16)

    v_ref[0] = jnp.dot(h, wv_ref[...], preferred_element_type=F32).astype(BF16)
    xr_ref[0] = jnp.dot(h, wx_ref[...], preferred_element_type=F32)
    y = jnp.dot(h, wy_ref[...], preferred_element_type=F32)
    yr_ref[0] = 0.5 * y * (1.0 + jnp.tanh(math.sqrt(2.0 / math.pi) * (y + 0.044715 * (y * y * y))))


def _in_proj(x, g, wq, wk, wv, wx, wy, gq, gk, cos, slo, shi, *, tm=512):
    b, s, d = x.shape
    tm = _tile(s, tm)
    tok = lambda w: pl.BlockSpec((1, tm, w), lambda i, j: (i, j, 0))
    tab = pl.BlockSpec((tm, HEAD_DIM), lambda i, j: (j, 0))
    return pl.pallas_call(
        _in_proj_kernel,
        out_shape=(
            jax.ShapeDtypeStruct((b, s, ATTN_WIDTH), BF16),
            jax.ShapeDtypeStruct((b, N_KV_HEADS, HEAD_DIM, s), BF16),
            jax.ShapeDtypeStruct((b, s, KV_WIDTH), BF16),
            jax.ShapeDtypeStruct((b, s, RNN_WIDTH), F32),
            jax.ShapeDtypeStruct((b, s, RNN_WIDTH), F32),
        ),
        grid=(b, s // tm),
        in_specs=[
            tok(d), _resident(g.shape),
            _resident(wq.shape), _resident(wk.shape), _resident(wv.shape),
            _resident(wx.shape), _resident(wy.shape),
            _resident(gq.shape), _resident(gk.shape),
            tab, tab, tab,
        ],
        out_specs=(
            tok(ATTN_WIDTH),
            pl.BlockSpec((1, N_KV_HEADS, HEAD_DIM, tm), lambda i, j: (i, 0, 0, j)),
            tok(KV_WIDTH), tok(RNN_WIDTH), tok(RNN_WIDTH),
        ),
        compiler_params=pltpu.CompilerParams(
            dimension_semantics=("parallel", "parallel"),
            vmem_limit_bytes=VMEM_LIMIT_BYTES),
        name="in_proj",
    )(x, g, wq, wk, wv, wx, wy, gq, gk, cos, slo, shi)


def _attn_kernel(q_ref, kt_ref, v_ref, qb_ref, o_ref, kmax_ref, *, tk_bounded, tk_online):
    tq = q_ref.shape[1]
    n_keys = kt_ref.shape[3]
    heads = [slice(hd * HEAD_DIM, (hd + 1) * HEAD_DIM) for hd in range(Q_PER_KV)]

    @pl.when(pl.program_id(2) == 0)
    def _():
        kf = kt_ref[0, 0].astype(F32)
        k2 = jnp.max(jnp.sum(kf * kf, axis=0, keepdims=True), axis=-1, keepdims=True)
        kmax_ref[...] = jnp.broadcast_to(jnp.sqrt(k2), kmax_ref.shape)

    m = qb_ref[:, :1] * kmax_ref[:, :1] * BOUND_SLACK
    bounded = 2.0 * jnp.max(m) <= SAFE_EXP2_RANGE

    def key_chunk(c, tk):
        start = pl.multiple_of(c * tk, tk)
        return kt_ref[0, 0, :, pl.ds(start, tk)], v_ref[0, pl.ds(start, tk), :]

    @pl.when(bounded)
    def _():
        def chunk(c, carry):
            kt, v = key_chunk(c, tk_bounded)
            out = []
            for sl, (l, acc) in zip(heads, carry):
                s = jnp.dot(q_ref[0, :, sl], kt, preferred_element_type=F32)
                p = jnp.exp2(s - m)
                l = l + jnp.sum(p, axis=-1, keepdims=True)
                acc = acc + jnp.dot(p.astype(BF16), v, preferred_element_type=F32)
                out.append((l, acc))
            return tuple(out)

        init = tuple((jnp.zeros((tq, 1), F32), jnp.zeros((tq, HEAD_DIM), F32)) for _ in heads)
        final = lax.fori_loop(0, n_keys // tk_bounded, chunk, init)
        for sl, (l, acc) in zip(heads, final):
            o_ref[0, :, sl] = acc / l

    @pl.when(jnp.logical_not(bounded))
    def _():
        def chunk(c, carry):
            kt, v = key_chunk(c, tk_online)
            out = []
            for sl, (m, l, acc) in zip(heads, carry):
                s = jnp.dot(q_ref[0, :, sl], kt, preferred_element_type=F32)
                m_new = jnp.maximum(m, jnp.max(s, axis=-1, keepdims=True))
                p = jnp.exp2(s - m_new)
                alpha = jnp.exp2(m - m_new)
                l = alpha * l + jnp.sum(p, axis=-1, keepdims=True)
                acc = alpha * acc + jnp.dot(p.astype(BF16), v, preferred_element_type=F32)
                out.append((m_new, l, acc))
            return tuple(out)

        init = tuple((jnp.full((tq, 1), -jnp.inf, F32), jnp.zeros((tq, 1), F32),
                      jnp.zeros((tq, HEAD_DIM), F32)) for _ in heads)
        final = lax.fori_loop(0, n_keys // tk_online, chunk, init)
        for sl, (_, l, acc) in zip(heads, final):
            o_ref[0, :, sl] = acc / l


def _attention(q, kt, v, q_bound, *, tq=1024, tk_bounded=4096, tk_online=2048):
    b, s, _ = q.shape
    tq = _tile(s, tq)
    tk_bounded = _tile(s, tk_bounded)
    tk_online = _tile(s, tk_online)
    gw = Q_PER_KV * HEAD_DIM
    return pl.pallas_call(
        functools.partial(_attn_kernel, tk_bounded=tk_bounded, tk_online=tk_online),
        out_shape=jax.ShapeDtypeStruct((b, s, ATTN_WIDTH), F32),
        grid=(b, N_KV_HEADS, s // tq),
        in_specs=[
            pl.BlockSpec((1, tq, gw), lambda i, g, j: (i, j, g)),
            pl.BlockSpec((1, 1, HEAD_DIM, s), lambda i, g, j: (i, g, 0, 0),
                         pipeline_mode=pl.Buffered(1)),
            pl.BlockSpec((1, s, HEAD_DIM), lambda i, g, j: (i, 0, g),
                         pipeline_mode=pl.Buffered(1)),
            _resident(q_bound.shape),
        ],
        out_specs=pl.BlockSpec((1, tq, gw), lambda i, g, j: (i, j, g)),
        scratch_shapes=[pltpu.VMEM((1, HEAD_DIM), F32)],
        compiler_params=pltpu.CompilerParams(
            dimension_semantics=("parallel", "parallel", "arbitrary"),
            vmem_limit_bytes=VMEM_LIMIT_BYTES),
        name="attention",
    )(q, kt, v, q_bound)


def _conv(row, x_ref, prev_ref, next_ref, wc_ref, bc_ref, ext_ref, xc_ref):
    t = pl.program_id(1)
    nt = pl.num_programs(1)
    tt = x_ref.shape[1]
    ext_ref[row, 0:SUBLANES] = jnp.where(t > 0, prev_ref[row], 0.0)
    ext_ref[row, SUBLANES:SUBLANES + tt] = x_ref[row]
    ext_ref[row, SUBLANES + tt:] = jnp.where(t < nt - 1, next_ref[row], 0.0)
    for n in range(RNN_BLOCKS):
        sl = slice(n * RNN_BLOCK_W, (n + 1) * RNN_BLOCK_W)
        ext = ext_ref[row, :, sl]
        xc = bc_ref[:, sl]
        for j in range(CONV_W):
            shift = (CONV_PAD_L - j) % ext.shape[0]
            tap = pltpu.roll(ext, shift, axis=0) if shift else ext
            xc = xc + tap[SUBLANES:SUBLANES + tt] * wc_ref[j:j + 1, sl]
        xc_ref[row, :, sl] = xc


def _gates(row, xc_ref, wg_ref, ba_ref, bx_ref, lam_ref, a_ref, u_ref):
    nl = -lam_ref[...]
    half_decay = (-0.5 * RG_C * LOG2E) * (jnp.maximum(nl, 0.0)
                                          + jnp.log(1.0 + jnp.exp(-jnp.abs(nl))))
    half_ba = 0.5 * ba_ref[...]
    half_bx = 0.5 * bx_ref[...]
    for n in range(RNN_BLOCKS):
        sl = slice(n * RNN_BLOCK_W, (n + 1) * RNN_BLOCK_W)
        xc = xc_ref[row, :, sl]
        gates = jnp.dot(xc.astype(BF16), wg_ref[n], preferred_element_type=F32)
        tr = jnp.tanh(gates[:, :RNN_BLOCK_W] + half_ba[:, sl])
        i = 0.5 * jnp.tanh(gates[:, RNN_BLOCK_W:] + half_bx[:, sl]) + 0.5
        a = jnp.exp2(half_decay[:, sl] * tr + half_decay[:, sl])
        a_ref[row, :, sl] = a
        v = 1.0 - a * a
        u_ref[row, :, sl] = jnp.where(v > 0.0, v * lax.rsqrt(v), 0.0) * (i * xc)


def _scan(a_ref, u_ref, o_ref, carry_ref, *, reverse):
    nb, tt, _ = a_ref.shape
    n_groups = tt // SCAN_GROUP

    def group(g, hs):
        first = (n_groups - 1 - g) * SCAN_GROUP if reverse else g * SCAN_GROUP
        rows = pl.ds(pl.multiple_of(first, SCAN_GROUP), SCAN_GROUP)
        views = [(a_ref.at[b, rows, :], u_ref.at[b, rows, :], o_ref.at[b, rows, :])
                 for b in range(nb)]
        hs = list(hs)
        for k in (reversed(range(SCAN_GROUP)) if reverse else range(SCAN_GROUP)):
            for b, (a_t, u_t, o_t) in enumerate(views):
                hs[b] = a_t[k:k + 1, :] * hs[b] + u_t[k:k + 1, :]
                o_t[k:k + 1, :] = hs[b]
        return tuple(hs)

    hs = lax.fori_loop(0, n_groups, group, tuple(carry_ref[b] for b in range(nb)))
    for b in range(nb):
        carry_ref[b] = hs[b]


def _rglru_fwd_kernel(x_ref, prev_ref, next_ref, wc_ref, bc_ref, wg_ref, ba_ref, bx_ref, lam_ref,
                      xc_ref, hf_ref, ext_ref, a_ref, u_ref, carry_ref):
    @pl.when(pl.program_id(1) == 0)
    def _():
        carry_ref[...] = jnp.zeros_like(carry_ref)

    for row in range(x_ref.shape[0]):
        _conv(row, x_ref, prev_ref, next_ref, wc_ref, bc_ref, ext_ref, xc_ref)
        _gates(row, xc_ref, wg_ref, ba_ref, bx_ref, lam_ref, a_ref, u_ref)
    _scan(a_ref, u_ref, hf_ref, carry_ref, reverse=False)


def _rglru_bwd_kernel(xc_ref, wg_ref, ba_ref, bx_ref, lam_ref, hf_ref, y_ref,
                      o_ref, a_ref, u_ref, hb_ref, carry_ref):
    @pl.when(pl.program_id(1) == 0)
    def _():
        carry_ref[...] = jnp.zeros_like(carry_ref)

    for row in range(xc_ref.shape[0]):
        _gates(row, xc_ref, wg_ref, ba_ref, bx_ref, lam_ref, a_ref, u_ref)
    _scan(a_ref, u_ref, hb_ref, carry_ref, reverse=True)

    for row in range(xc_ref.shape[0]):
        o_ref[row] = (hf_ref[row] + hb_ref[row]) * y_ref[row]


def _rglru(xr, yr, wc, bc, wg_f, ba_f, bx_f, lam_f, wg_b, ba_b, bx_b, lam_b, *, tt=512):
    b, s, r = xr.shape
    tt = _tile(s, tt)
    nt = s // tt
    nb = _tile(b, RNN_BATCH_ROWS)
    groups = tt // SUBLANES
    last_group = s // SUBLANES - 1
    gate_params = [_resident(wg_f.shape), _resident(ba_f.shape), _resident(bx_f.shape),
                   _resident(lam_f.shape)]
    chunk = (nb, tt, r)
    cparams = pltpu.CompilerParams(dimension_semantics=("parallel", "arbitrary"),
                                   vmem_limit_bytes=VMEM_LIMIT_BYTES)
    full = jax.ShapeDtypeStruct((b, s, r), F32)

    cur = pl.BlockSpec(chunk, lambda i, j: (i, j, 0))
    prev = pl.BlockSpec((nb, SUBLANES, r), lambda i, j: (i, jnp.maximum(j * groups - 1, 0), 0))
    nxt = pl.BlockSpec((nb, SUBLANES, r),
                       lambda i, j: (i, jnp.minimum((j + 1) * groups, last_group), 0))
    xc, hf = pl.pallas_call(
        _rglru_fwd_kernel,
        out_shape=(full, full),
        grid=(b // nb, nt),
        in_specs=[cur, prev, nxt, _resident(wc.shape), _resident(bc.shape)] + gate_params,
        out_specs=(cur, cur),
        scratch_shapes=[pltpu.VMEM((nb, tt + 2 * SUBLANES, r), F32), pltpu.VMEM(chunk, F32),
                        pltpu.VMEM(chunk, F32), pltpu.VMEM((nb, 1, r), F32)],
        compiler_params=cparams,
        name="rglru_fwd",
    )(xr, xr, xr, wc, bc, wg_f, ba_f, bx_f, lam_f)

    rev = pl.BlockSpec(chunk, lambda i, j: (i, nt - 1 - j, 0))
    return pl.pallas_call(
        _rglru_bwd_kernel,
        out_shape=full,
        grid=(b // nb, nt),
        in_specs=[rev] + gate_params + [rev, rev],
        out_specs=rev,
        scratch_shapes=[pltpu.VMEM(chunk, F32), pltpu.VMEM(chunk, F32), pltpu.VMEM(chunk, F32),
                        pltpu.VMEM((nb, 1, r), F32)],
        compiler_params=cparams,
        name="rglru_bwd",
    )(xc, wg_b, ba_b, bx_b, lam_b, hf, yr)


def _out_proj_kernel(x_ref, a_ref, r_ref, ga_ref, gr_ref, wa_ref, wr_ref, gp_ref, o_ref):
    an = _rms(a_ref[...], ga_ref[...]).astype(BF16)
    rn = _rms(r_ref[...], gr_ref[...]).astype(BF16)
    y = (jnp.dot(an, wa_ref[...], preferred_element_type=F32)
         + jnp.dot(rn, wr_ref[...], preferred_element_type=F32))
    o_ref[...] = x_ref[...] + _rms(y, gp_ref[...])


def _out_proj(x, attn, rnn, ga, gr, wa, wr, gp, *, tm=512):
    t, d = x.shape
    tm = _tile(t, tm)
    tok = lambda w: pl.BlockSpec((tm, w), lambda i: (i, 0))
    return pl.pallas_call(
        _out_proj_kernel,
        out_shape=jax.ShapeDtypeStruct((t, d), F32),
        grid=(t // tm,),
        in_specs=[tok(d), tok(ATTN_WIDTH), tok(RNN_WIDTH),
                  _resident(ga.shape), _resident(gr.shape),
                  _resident(wa.shape), _resident(wr.shape), _resident(gp.shape)],
        out_specs=tok(d),
        compiler_params=pltpu.CompilerParams(
            dimension_semantics=("parallel",),
            vmem_limit_bytes=VMEM_LIMIT_BYTES),
        name="out_proj",
    )(x, attn, rnn, ga, gr, wa, wr, gp)


def _rope_tables(s):
    n_rows = s // GRID_W
    lane = jnp.arange(HEAD_DIM)
    inv_freq = ROPE_THETA ** (-jnp.arange(ROPE_PAIRS, dtype=F32) / ROPE_PAIRS)
    inv_lane = inv_freq[lane % ROPE_PAIRS]
    uses_col = lane >= 2 * ROPE_PAIRS
    upper = (lane % (2 * ROPE_PAIRS)) >= ROPE_PAIRS
    ang_r = (jnp.arange(n_rows, dtype=F32)[:, None] * inv_lane)[:, None, :]
    ang_c = (jnp.arange(GRID_W, dtype=F32)[:, None] * inv_lane)[None, :, :]
    cos = jnp.where(uses_col, jnp.cos(ang_c), jnp.cos(ang_r)).reshape(s, HEAD_DIM)
    sin = jnp.where(uses_col, jnp.sin(ang_c), jnp.sin(ang_r)).reshape(s, HEAD_DIM)
    sin_lo = jnp.where(upper, 0.0, -sin)
    sin_hi = jnp.where(upper, sin, 0.0)
    return cos, sin_lo, sin_hi


def _prepare(p):
    row = lambda v: v.reshape(1, -1)
    w_in = p["w_in"].astype(BF16)
    c0, c1, c2, c3 = ATTN_WIDTH, ATTN_WIDTH + KV_WIDTH, ATTN_WIDTH + 2 * KV_WIDTH, \
        ATTN_WIDTH + 2 * KV_WIDTH + RNN_WIDTH
    w_out = p["w_out"].astype(BF16)
    out = dict(
        ffn1=(row(p["g_ffn1_pre"]), p["w_ffn1_gate"].astype(BF16), p["w_ffn1_up"].astype(BF16),
              p["w_ffn1_down"].astype(BF16), row(p["g_ffn1_post"])),
        ffn2=(row(p["g_ffn2_pre"]), p["w_ffn2_gate"].astype(BF16), p["w_ffn2_up"].astype(BF16),
              p["w_ffn2_down"].astype(BF16), row(p["g_ffn2_post"])),
        in_proj=(row(p["g_mix_pre"]), w_in[:, :c0], w_in[:, c0:c1], w_in[:, c1:c2],
                 w_in[:, c2:c3], w_in[:, c3:], row(p["g_q"]), row(p["g_k"])),
        q_bound=jnp.full((1, HEAD_DIM), LOG2E, F32) * jnp.max(jnp.abs(p["g_q"])),
        rglru=(p["w_conv"], row(p["b_conv"]),
               (0.5 * jnp.concatenate([p["w_a_fwd"], p["w_x_fwd"]], axis=-1)).astype(BF16),
               row(p["b_a_fwd"]), row(p["b_x_fwd"]), row(p["lam_fwd"]),
               (0.5 * jnp.concatenate([p["w_a_bwd"], p["w_x_bwd"]], axis=-1)).astype(BF16),
               row(p["b_a_bwd"]), row(p["b_x_bwd"]), row(p["lam_bwd"])),
        out_proj=(row(p["g_attn_out"]), row(p["g_rnn_out"]), w_out[:ATTN_WIDTH], w_out[ATTN_WIDTH:],
                  row(p["g_mix_post"])),
    )
    return out


def _layer(x, w, rope):
    b, s, d = x.shape
    x1 = _ffn(x.reshape(b * s, d), *w["ffn1"])
    q, kt, v, xr, yr = _in_proj(x1.reshape(b, s, d), *w["in_proj"], *rope)
    attn = _attention(q, kt, v, w["q_bound"])
    rnn = _rglru(xr, yr, *w["rglru"])
    x2 = _out_proj(x1, attn.reshape(b * s, -1), rnn.reshape(b * s, -1), *w["out_proj"])
    x3 = _ffn(x2, *w["ffn2"])
    return x3.reshape(b, s, d)


_PARAM_NAMES = (
    "g_ffn1_pre", "w_ffn1_gate", "w_ffn1_up", "w_ffn1_down", "g_ffn1_post",
    "g_mix_pre", "w_in", "g_q", "g_k", "w_conv", "b_conv",
    "w_a_fwd", "b_a_fwd", "w_x_fwd", "b_x_fwd", "lam_fwd",
    "w_a_bwd", "b_a_bwd", "w_x_bwd", "b_x_bwd", "lam_bwd",
    "g_attn_out", "g_rnn_out", "w_out", "g_mix_post",
    "g_ffn2_pre", "w_ffn2_gate", "w_ffn2_up", "w_ffn2_down", "g_ffn2_post")


def kernel(x_prompt, x_sample, g_ffn1_pre, w_ffn1_gate, w_ffn1_up, w_ffn1_down, g_ffn1_post, g_mix_pre, w_in, g_q, g_k, w_conv, b_conv, w_a_fwd, b_a_fwd, w_x_fwd, b_x_fwd, lam_fwd, w_a_bwd, b_a_bwd, w_x_bwd, b_x_bwd, lam_bwd, g_attn_out, g_rnn_out, w_out, g_mix_post, g_ffn2_pre, w_ffn2_gate, w_ffn2_up, w_ffn2_down, g_ffn2_post):
    stacked = (g_ffn1_pre, w_ffn1_gate, w_ffn1_up, w_ffn1_down, g_ffn1_post, g_mix_pre, w_in, g_q,
               g_k, w_conv, b_conv, w_a_fwd, b_a_fwd, w_x_fwd, b_x_fwd, lam_fwd, w_a_bwd, b_a_bwd,
               w_x_bwd, b_x_bwd, lam_bwd, g_attn_out, g_rnn_out, w_out, g_mix_post, g_ffn2_pre,
               w_ffn2_gate, w_ffn2_up, w_ffn2_down, g_ffn2_post)
    depth = g_ffn1_pre.shape[0]
    y_prompt, y_sample = x_prompt, x_sample
    rope = _rope_tables(max(x_prompt.shape[1], x_sample.shape[1]))
    for layer in range(depth):
        w = _prepare({n: a[layer] for n, a in zip(_PARAM_NAMES, stacked)})
        y_prompt = _layer(y_prompt, w, rope)
        y_sample = _layer(y_sample, w, rope)
    return (y_prompt, y_sample)
```

```python
import functools
import math

import jax
import jax.numpy as jnp
from jax import lax
from jax.experimental import pallas as pl
from jax.experimental.pallas import tpu as pltpu

F32 = jnp.float32
BF16 = jnp.bfloat16

EPS = 1e-6
HEAD_DIM = 128
N_Q_HEADS = 8
N_KV_HEADS = 2
Q_PER_KV = N_Q_HEADS // N_KV_HEADS
ATTN_WIDTH = N_Q_HEADS * HEAD_DIM
KV_WIDTH = N_KV_HEADS * HEAD_DIM
RNN_BLOCKS = 8
RNN_BLOCK_W = 128
RNN_WIDTH = RNN_BLOCKS * RNN_BLOCK_W
GRID_W = 64
ROPE_PAIRS = HEAD_DIM // 4
ROPE_THETA = 10000.0
CONV_W = 4
CONV_PAD_L = 2
RG_C = 8.0
LOG2E = math.log2(math.e)
SUBLANES = 8
SCAN_GROUP = 16
RNN_BATCH_ROWS = 2
FFN_TF = 512
NORM_ROWS = 32
SAFE_EXP2_RANGE = 100.0
BOUND_SLACK = 1.0 + 2.0 ** -6

VMEM_LIMIT_BYTES = 58 * 1024 * 1024


def _tile(n, want):
    t = min(n, want)
    assert n % t == 0, (n, want)
    return t


def _rms(x, g):
    ms = jnp.mean(x * x, axis=-1, keepdims=True)
    return x * lax.rsqrt(ms + EPS) * g


def _silu(x):
    h = 0.5 * x
    return h * jnp.tanh(h) + h


def _resident(shape):
    nd = len(shape)
    return pl.BlockSpec(shape, lambda *_: (0,) * nd, pipeline_mode=pl.Buffered(1))


def _ffn_kernel(x_ref, gpre_ref, wg_ref, wu_ref, wd_ref, gpost_ref, o_ref, h_ref, *, row_split):
    j = pl.program_id(1)
    last = pl.num_programs(1) - 1
    tm, _ = x_ref.shape
    hm = tm // row_split

    def step(is_first, is_last):
        gp = 0.5 * gpost_ref[...]
        for r in range(row_split):
            rows = slice(r * hm, (r + 1) * hm)
            chunks = [slice(r * hm + i * NORM_ROWS, r * hm + (i + 1) * NORM_ROWS)
                      for i in range(hm // NORM_ROWS)]
            if is_first:
                for c in chunks:
                    h_ref[c, :] = _rms(x_ref[c, :], gpre_ref[...]).astype(BF16)
            h = h_ref[rows, :]
            g = jnp.dot(h, wg_ref[...], preferred_element_type=F32)
            u = jnp.dot(h, wu_ref[...], preferred_element_type=F32)
            a = (_silu(g) * u).astype(BF16)
            down = jnp.dot(a, wd_ref[...], preferred_element_type=F32)
            if is_first:
                o_ref[rows, :] = down
            else:
                o_ref[rows, :] += down
            if is_last:
                for c in chunks:
                    o_ref[c, :] = x_ref[c, :] + _rms(o_ref[c, :], gp)

    pl.when(j == 0)(lambda: step(True, False))
    pl.when(jnp.logical_and(j > 0, j < last))(lambda: step(False, False))
    pl.when(j == last)(lambda: step(False, True))


def _ffn(x, g_pre, wg, wu, wd, g_post, *, tm=1024):
    t, d = x.shape
    nf, _, tf = wg.shape
    f = nf * tf
    tm = _tile(t, tm)
    assert f // tf >= 2, "first and last d_ff steps must be distinct grid steps"
    row_split = 2 if tm % (2 * NORM_ROWS) == 0 else 1
    return pl.pallas_call(
        functools.partial(_ffn_kernel, row_split=row_split),
        out_shape=jax.ShapeDtypeStruct((t, d), F32),
        grid=(t // tm, f // tf),
        in_specs=[
            pl.BlockSpec((tm, d), lambda i, j: (i, 0)),
            pl.BlockSpec((1, d), lambda i, j: (0, 0)),
            pl.BlockSpec((None, d, tf), lambda i, j: (j, 0, 0)),
            pl.BlockSpec((None, d, tf), lambda i, j: (j, 0, 0)),
            pl.BlockSpec((tf, d), lambda i, j: (j, 0)),
            pl.BlockSpec((1, d), lambda i, j: (0, 0)),
        ],
        out_specs=pl.BlockSpec((tm, d), lambda i, j: (i, 0)),
        scratch_shapes=[pltpu.VMEM((tm, d), BF16)],
        compiler_params=pltpu.CompilerParams(
            dimension_semantics=("parallel", "arbitrary"),
            vmem_limit_bytes=VMEM_LIMIT_BYTES),
        name="ffn",
    )(x, g_pre, wg, wu, wd, g_post)


def _rope(xn, cos, sin_lo, sin_hi):
    return (xn * cos
            + pltpu.roll(xn, HEAD_DIM - ROPE_PAIRS, axis=1) * sin_lo
            + pltpu.roll(xn, ROPE_PAIRS, axis=1) * sin_hi)


def _in_proj_kernel(x_ref, g_ref, wq_ref, wk_ref, wv_ref, wx_ref, wy_ref, gq_ref, gk_ref,
                    cos_ref, slo_ref, shi_ref,
                    q_ref, kt_ref, v_ref, xr_ref, yr_ref):
    h = _rms(x_ref[0], g_ref[...]).astype(BF16)
    cos = cos_ref[...]
    slo = slo_ref[...]
    shi = shi_ref[...]

    q = jnp.dot(h, wq_ref[...], preferred_element_type=F32)
    gq = gq_ref[...] * (HEAD_DIM ** -0.5 * LOG2E)
    for hd in range(N_Q_HEADS):
        sl = slice(hd * HEAD_DIM, (hd + 1) * HEAD_DIM)
        q_ref[0, :, sl] = _rope(_rms(q[:, sl], gq), cos, slo, shi).astype(BF16)

    k = jnp.dot(h, wk_ref[...], preferred_element_type=F32)
    for hd in range(N_KV_HEADS):
        sl = slice(hd * HEAD_DIM, (hd + 1) * HEAD_DIM)
        kr = _rope(_rms(k[:, sl], gk_ref[...]), cos, slo, shi)
        kt_ref[0, hd] = kr.T.astype(BF16)

    v_ref[0] = jnp.dot(h, wv_ref[...], preferred_element_type=F32).astype(BF16)
    xr_ref[0] = jnp.dot(h, wx_ref[...], preferred_element_type=F32)
    yr_ref[0] = jnp.dot(h, wy_ref[...], preferred_element_type=F32)


def _in_proj(x, g, wq, wk, wv, wx, wy, gq, gk, cos, slo, shi, *, tm=512):
    b, s, d = x.shape
    tm = _tile(s, tm)
    tok = lambda w: pl.BlockSpec((1, tm, w), lambda i, j: (i, j, 0))
    tab = pl.BlockSpec((tm, HEAD_DIM), lambda i, j: (j, 0))
    return pl.pallas_call(
        _in_proj_kernel,
        out_shape=(
            jax.ShapeDtypeStruct((b, s, ATTN_WIDTH), BF16),
            jax.ShapeDtypeStruct((b, N_KV_HEADS, HEAD_DIM, s), BF16),
            jax.ShapeDtypeStruct((b, s, KV_WIDTH), BF16),
            jax.ShapeDtypeStruct((b, s, RNN_WIDTH), F32),
            jax.ShapeDtypeStruct((b, s, RNN_WIDTH), F32),
        ),
        grid=(b, s // tm),
        in_specs=[
            tok(d), _resident(g.shape),
            _resident(wq.shape), _resident(wk.shape), _resident(wv.shape),
            _resident(wx.shape), _resident(wy.shape),
            _resident(gq.shape), _resident(gk.shape),
            tab, tab, tab,
        ],
        out_specs=(
            tok(ATTN_WIDTH),
            pl.BlockSpec((1, N_KV_HEADS, HEAD_DIM, tm), lambda i, j: (i, 0, 0, j)),
            tok(KV_WIDTH), tok(RNN_WIDTH), tok(RNN_WIDTH),
        ),
        compiler_params=pltpu.CompilerParams(
            dimension_semantics=("parallel", "parallel"),
            vmem_limit_bytes=VMEM_LIMIT_BYTES),
        name="in_proj",
    )(x, g, wq, wk, wv, wx, wy, gq, gk, cos, slo, shi)


def _attn_kernel(q_ref, kt_ref, v_ref, qb_ref, o_ref, kmax_ref, *, tk_bounded, tk_online):
    tq = q_ref.shape[1]
    n_keys = kt_ref.shape[3]
    heads = [slice(hd * HEAD_DIM, (hd + 1) * HEAD_DIM) for hd in range(Q_PER_KV)]

    @pl.when(pl.program_id(2) == 0)
    def _():
        kf = kt_ref[0, 0].astype(F32)
        k2 = jnp.max(jnp.sum(kf * kf, axis=0, keepdims=True), axis=-1, keepdims=True)
        kmax_ref[...] = jnp.broadcast_to(jnp.sqrt(k2), kmax_ref.shape)

    m = qb_ref[:, :1] * kmax_ref[:, :1] * BOUND_SLACK
    bounded = 2.0 * jnp.max(m) <= SAFE_EXP2_RANGE

    def key_chunk(c, tk):
        start = pl.multiple_of(c * tk, tk)
        return kt_ref[0, 0, :, pl.ds(start, tk)], v_ref[0, pl.ds(start, tk), :]

    @pl.when(bounded)
    def _():
        def chunk(c, carry):
            kt, v = key_chunk(c, tk_bounded)
            out = []
            for sl, (l, acc) in zip(heads, carry):
                s = jnp.dot(q_ref[0, :, sl], kt, preferred_element_type=F32)
                p = jnp.exp2(s - m)
                l = l + jnp.sum(p, axis=-1, keepdims=True)
                acc = acc + jnp.dot(p.astype(BF16), v, preferred_element_type=F32)
                out.append((l, acc))
            return tuple(out)

        init = tuple((jnp.zeros((tq, 1), F32), jnp.zeros((tq, HEAD_DIM), F32)) for _ in heads)
        final = lax.fori_loop(0, n_keys // tk_bounded, chunk, init)
        for sl, (l, acc) in zip(heads, final):
            o_ref[0, :, sl] = acc / l

    @pl.when(jnp.logical_not(bounded))
    def _():
        def chunk(c, carry):
            kt, v = key_chunk(c, tk_online)
            out = []
            for sl, (m, l, acc) in zip(heads, carry):
                s = jnp.dot(q_ref[0, :, sl], kt, preferred_element_type=F32)
                m_new = jnp.maximum(m, jnp.max(s, axis=-1, keepdims=True))
                p = jnp.exp2(s - m_new)
                alpha = jnp.exp2(m - m_new)
                l = alpha * l + jnp.sum(p, axis=-1, keepdims=True)
                acc = alpha * acc + jnp.dot(p.astype(BF16), v, preferred_element_type=F32)
                out.append((m_new, l, acc))
            return tuple(out)

        init = tuple((jnp.full((tq, 1), -jnp.inf, F32), jnp.zeros((tq, 1), F32),
                      jnp.zeros((tq, HEAD_DIM), F32)) for _ in heads)
        final = lax.fori_loop(0, n_keys // tk_online, chunk, init)
        for sl, (_, l, acc) in zip(heads, final):
            o_ref[0, :, sl] = acc / l


def _attention(q, kt, v, q_bound, *, tq=1024, tk_bounded=4096, tk_online=2048):
    b, s, _ = q.shape
    tq = _tile(s, tq)
    tk_bounded = _tile(s, tk_bounded)
    tk_online = _tile(s, tk_online)
    gw = Q_PER_KV * HEAD_DIM
    return pl.pallas_call(
        functools.partial(_attn_kernel, tk_bounded=tk_bounded, tk_online=tk_online),
        out_shape=jax.ShapeDtypeStruct((b, s, ATTN_WIDTH), F32),
        grid=(b, N_KV_HEADS, s // tq),
        in_specs=[
            pl.BlockSpec((1, tq, gw), lambda i, g, j: (i, j, g)),
            pl.BlockSpec((1, 1, HEAD_DIM, s), lambda i, g, j: (i, g, 0, 0),
                         pipeline_mode=pl.Buffered(1)),
            pl.BlockSpec((1, s, HEAD_DIM), lambda i, g, j: (i, 0, g),
                         pipeline_mode=pl.Buffered(1)),
            _resident(q_bound.shape),
        ],
        out_specs=pl.BlockSpec((1, tq, gw), lambda i, g, j: (i, j, g)),
        scratch_shapes=[pltpu.VMEM((1, HEAD_DIM), F32)],
        compiler_params=pltpu.CompilerParams(
            dimension_semantics=("parallel", "parallel", "arbitrary"),
            vmem_limit_bytes=VMEM_LIMIT_BYTES),
        name="attention",
    )(q, kt, v, q_bound)


def _conv(row, x_ref, prev_ref, next_ref, wc_ref, bc_ref, ext_ref, xc_ref):
    t = pl.program_id(1)
    nt = pl.num_programs(1)
    tt = x_ref.shape[1]
    ext_ref[row, 0:SUBLANES] = jnp.where(t > 0, prev_ref[row], 0.0)
    ext_ref[row, SUBLANES:SUBLANES + tt] = x_ref[row]
    ext_ref[row, SUBLANES + tt:] = jnp.where(t < nt - 1, next_ref[row], 0.0)
    for n in range(RNN_BLOCKS):
        sl = slice(n * RNN_BLOCK_W, (n + 1) * RNN_BLOCK_W)
        ext = ext_ref[row, :, sl]
        xc = bc_ref[:, sl]
        for j in range(CONV_W):
            shift = (CONV_PAD_L - j) % ext.shape[0]
            tap = pltpu.roll(ext, shift, axis=0) if shift else ext
            xc = xc + tap[SUBLANES:SUBLANES + tt] * wc_ref[j:j + 1, sl]
        xc_ref[row, :, sl] = xc


def _gates(row, xc_ref, wg_ref, ba_ref, bx_ref, lam_ref, a_ref, u_ref):
    nl = -lam_ref[...]
    half_decay = (-0.5 * RG_C * LOG2E) * (jnp.maximum(nl, 0.0)
                                          + jnp.log(1.0 + jnp.exp(-jnp.abs(nl))))
    half_ba = 0.5 * ba_ref[...]
    half_bx = 0.5 * bx_ref[...]
    for n in range(RNN_BLOCKS):
        sl = slice(n * RNN_BLOCK_W, (n + 1) * RNN_BLOCK_W)
        xc = xc_ref[row, :, sl]
        gates = jnp.dot(xc.astype(BF16), wg_ref[n], preferred_element_type=F32)
        tr = jnp.tanh(gates[:, :RNN_BLOCK_W] + half_ba[:, sl])
        i = 0.5 * jnp.tanh(gates[:, RNN_BLOCK_W:] + half_bx[:, sl]) + 0.5
        a = jnp.exp2(half_decay[:, sl] * tr + half_decay[:, sl])
        a_ref[row, :, sl] = a
        v = 1.0 - a * a
        u_ref[row, :, sl] = jnp.where(v > 0.0, v * lax.rsqrt(v), 0.0) * (i * xc)


def _scan(a_ref, u_ref, o_ref, carry_ref, *, reverse):
    nb, tt, _ = a_ref.shape
    n_groups = tt // SCAN_GROUP

    def group(g, hs):
        first = (n_groups - 1 - g) * SCAN_GROUP if reverse else g * SCAN_GROUP
        rows = pl.ds(pl.multiple_of(first, SCAN_GROUP), SCAN_GROUP)
        views = [(a_ref.at[b, rows, :], u_ref.at[b, rows, :], o_ref.at[b, rows, :])
                 for b in range(nb)]
        hs = list(hs)
        for k in (reversed(range(SCAN_GROUP)) if reverse else range(SCAN_GROUP)):
            for b, (a_t, u_t, o_t) in enumerate(views):
                hs[b] = a_t[k:k + 1, :] * hs[b] + u_t[k:k + 1, :]
                o_t[k:k + 1, :] = hs[b]
        return tuple(hs)

    hs = lax.fori_loop(0, n_groups, group, tuple(carry_ref[b] for b in range(nb)))
    for b in range(nb):
        carry_ref[b] = hs[b]


def _rglru_fwd_kernel(x_ref, prev_ref, next_ref, wc_ref, bc_ref, wg_ref, ba_ref, bx_ref, lam_ref,
                      xc_ref, hf_ref, ext_ref, a_ref, u_ref, carry_ref):
    @pl.when(pl.program_id(1) == 0)
    def _():
        carry_ref[...] = jnp.zeros_like(carry_ref)

    for row in range(x_ref.shape[0]):
        _conv(row, x_ref, prev_ref, next_ref, wc_ref, bc_ref, ext_ref, xc_ref)
        _gates(row, xc_ref, wg_ref, ba_ref, bx_ref, lam_ref, a_ref, u_ref)
    _scan(a_ref, u_ref, hf_ref, carry_ref, reverse=False)


def _rglru_bwd_kernel(xc_ref, wg_ref, ba_ref, bx_ref, lam_ref, hf_ref, y_ref,
                      o_ref, a_ref, u_ref, hb_ref, carry_ref):
    @pl.when(pl.program_id(1) == 0)
    def _():
        carry_ref[...] = jnp.zeros_like(carry_ref)

    for row in range(xc_ref.shape[0]):
        _gates(row, xc_ref, wg_ref, ba_ref, bx_ref, lam_ref, a_ref, u_ref)
    _scan(a_ref, u_ref, hb_ref, carry_ref, reverse=True)

    for row in range(xc_ref.shape[0]):
        y = y_ref[row]
        gelu = 0.5 * y * (1.0 + jnp.tanh(math.sqrt(2.0 / math.pi) * (y + 0.044715 * (y * y * y))))
        o_ref[row] = (hf_ref[row] + hb_ref[row]) * gelu


def _rglru(xr, yr, wc, bc, wg_f, ba_f, bx_f, lam_f, wg_b, ba_b, bx_b, lam_b, *, tt=512):
    b, s, r = xr.shape
    tt = _tile(s, tt)
    nt = s // tt
    nb = _tile(b, RNN_BATCH_ROWS)
    groups = tt // SUBLANES
    last_group = s // SUBLANES - 1
    gate_params = [_resident(wg_f.shape), _resident(ba_f.shape), _resident(bx_f.shape),
                   _resident(lam_f.shape)]
    chunk = (nb, tt, r)
    cparams = pltpu.CompilerParams(dimension_semantics=("parallel", "arbitrary"),
                                   vmem_limit_bytes=VMEM_LIMIT_BYTES)
    full = jax.ShapeDtypeStruct((b, s, r), F32)

    cur = pl.BlockSpec(chunk, lambda i, j: (i, j, 0))
    prev = pl.BlockSpec((nb, SUBLANES, r), lambda i, j: (i, jnp.maximum(j * groups - 1, 0), 0))
    nxt = pl.BlockSpec((nb, SUBLANES, r),
                       lambda i, j: (i, jnp.minimum((j + 1) * groups, last_group), 0))
    xc, hf = pl.pallas_call(
        _rglru_fwd_kernel,
        out_shape=(full, full),
        grid=(b // nb, nt),
        in_specs=[cur, prev, nxt, _resident(wc.shape), _resident(bc.shape)] + gate_params,
        out_specs=(cur, cur),
        scratch_shapes=[pltpu.VMEM((nb, tt + 2 * SUBLANES, r), F32), pltpu.VMEM(chunk, F32),
                        pltpu.VMEM(chunk, F32), pltpu.VMEM((nb, 1, r), F32)],
        compiler_params=cparams,
        name="rglru_fwd",
    )(xr, xr, xr, wc, bc, wg_f, ba_f, bx_f, lam_f)

    rev = pl.BlockSpec(chunk, lambda i, j: (i, nt - 1 - j, 0))
    return pl.pallas_call(
        _rglru_bwd_kernel,
        out_shape=full,
        grid=(b // nb, nt),
        in_specs=[rev] + gate_params + [rev, rev],
        out_specs=rev,
        scratch_shapes=[pltpu.VMEM(chunk, F32), pltpu.VMEM(chunk, F32), pltpu.VMEM(chunk, F32),
                        pltpu.VMEM((nb, 1, r), F32)],
        compiler_params=cparams,
        name="rglru_bwd",
    )(xc, wg_b, ba_b, bx_b, lam_b, hf, yr)


def _out_proj_kernel(x_ref, a_ref, r_ref, ga_ref, gr_ref, wa_ref, wr_ref, gp_ref, o_ref):
    an = _rms(a_ref[...], ga_ref[...]).astype(BF16)
    rn = _rms(r_ref[...], gr_ref[...]).astype(BF16)
    y = (jnp.dot(an, wa_ref[...], preferred_element_type=F32)
         + jnp.dot(rn, wr_ref[...], preferred_element_type=F32))
    o_ref[...] = x_ref[...] + _rms(y, gp_ref[...])


def _out_proj(x, attn, rnn, ga, gr, wa, wr, gp, *, tm=512):
    t, d = x.shape
    tm = _tile(t, tm)
    tok = lambda w: pl.BlockSpec((tm, w), lambda i: (i, 0))
    return pl.pallas_call(
        _out_proj_kernel,
        out_shape=jax.ShapeDtypeStruct((t, d), F32),
        grid=(t // tm,),
        in_specs=[tok(d), tok(ATTN_WIDTH), tok(RNN_WIDTH),
                  _resident(ga.shape), _resident(gr.shape),
                  _resident(wa.shape), _resident(wr.shape), _resident(gp.shape)],
        out_specs=tok(d),
        compiler_params=pltpu.CompilerParams(
            dimension_semantics=("parallel",),
            vmem_limit_bytes=VMEM_LIMIT_BYTES),
        name="out_proj",
    )(x, attn, rnn, ga, gr, wa, wr, gp)


def _rope_tables(s):
    n_rows = s // GRID_W
    lane = jnp.arange(HEAD_DIM)
    inv_freq = ROPE_THETA ** (-jnp.arange(ROPE_PAIRS, dtype=F32) / ROPE_PAIRS)
    inv_lane = inv_freq[lane % ROPE_PAIRS]
    uses_col = lane >= 2 * ROPE_PAIRS
    upper = (lane % (2 * ROPE_PAIRS)) >= ROPE_PAIRS
    ang_r = (jnp.arange(n_rows, dtype=F32)[:, None] * inv_lane)[:, None, :]
    ang_c = (jnp.arange(GRID_W, dtype=F32)[:, None] * inv_lane)[None, :, :]
    cos = jnp.where(uses_col, jnp.cos(ang_c), jnp.cos(ang_r)).reshape(s, HEAD_DIM)
    sin = jnp.where(uses_col, jnp.sin(ang_c), jnp.sin(ang_r)).reshape(s, HEAD_DIM)
    sin_lo = jnp.where(upper, 0.0, -sin)
    sin_hi = jnp.where(upper, sin, 0.0)
    return cos, sin_lo, sin_hi


def _prepare(p):
    row = lambda v: v.reshape(1, -1)

    def col_blocks(w):
        d, f = w.shape
        return w.astype(BF16).reshape(d, f // FFN_TF, FFN_TF).transpose(1, 0, 2)

    w_in = p["w_in"].astype(BF16)
    c0, c1, c2, c3 = ATTN_WIDTH, ATTN_WIDTH + KV_WIDTH, ATTN_WIDTH + 2 * KV_WIDTH, \
        ATTN_WIDTH + 2 * KV_WIDTH + RNN_WIDTH
    w_out = p["w_out"].astype(BF16)
    out = dict(
        ffn1=(row(p["g_ffn1_pre"]), col_blocks(p["w_ffn1_gate"]), col_blocks(p["w_ffn1_up"]),
              p["w_ffn1_down"].astype(BF16), row(p["g_ffn1_post"])),
        ffn2=(row(p["g_ffn2_pre"]), col_blocks(p["w_ffn2_gate"]), col_blocks(p["w_ffn2_up"]),
              p["w_ffn2_down"].astype(BF16), row(p["g_ffn2_post"])),
        in_proj=(row(p["g_mix_pre"]), w_in[:, :c0], w_in[:, c0:c1], w_in[:, c1:c2],
                 w_in[:, c2:c3], w_in[:, c3:], row(p["g_q"]), row(p["g_k"])),
        q_bound=jnp.full((1, HEAD_DIM), LOG2E, F32) * jnp.max(jnp.abs(p["g_q"])),
        rglru=(p["w_conv"], row(p["b_conv"]),
               (0.5 * jnp.concatenate([p["w_a_fwd"], p["w_x_fwd"]], axis=-1)).astype(BF16),
               row(p["b_a_fwd"]), row(p["b_x_fwd"]), row(p["lam_fwd"]),
               (0.5 * jnp.concatenate([p["w_a_bwd"], p["w_x_bwd"]], axis=-1)).astype(BF16),
               row(p["b_a_bwd"]), row(p["b_x_bwd"]), row(p["lam_bwd"])),
        out_proj=(row(p["g_attn_out"]), row(p["g_rnn_out"]), w_out[:ATTN_WIDTH], w_out[ATTN_WIDTH:],
                  row(p["g_mix_post"])),
    )
    return out


def _layer(x, w, rope):
    b, s, d = x.shape
    x1 = _ffn(x.reshape(b * s, d), *w["ffn1"])
    q, kt, v, xr, yr = _in_proj(x1.reshape(b, s, d), *w["in_proj"], *rope)
    attn = _attention(q, kt, v, w["q_bound"])
    rnn = _rglru(xr, yr, *w["rglru"])
    x2 = _out_proj(x1, attn.reshape(b * s, -1), rnn.reshape(b * s, -1), *w["out_proj"])
    x3 = _ffn(x2, *w["ffn2"])
    return x3.reshape(b, s, d)


_PARAM_NAMES = (
    "g_ffn1_pre", "w_ffn1_gate", "w_ffn1_up", "w_ffn1_down", "g_ffn1_post",
    "g_mix_pre", "w_in", "g_q", "g_k", "w_conv", "b_conv",
    "w_a_fwd", "b_a_fwd", "w_x_fwd", "b_x_fwd", "lam_fwd",
    "w_a_bwd", "b_a_bwd", "w_x_bwd", "b_x_bwd", "lam_bwd",
    "g_attn_out", "g_rnn_out", "w_out", "g_mix_post",
    "g_ffn2_pre", "w_ffn2_gate", "w_ffn2_up", "w_ffn2_down", "g_ffn2_post")


def kernel(x_prompt, x_sample, g_ffn1_pre, w_ffn1_gate, w_ffn1_up, w_ffn1_down, g_ffn1_post, g_mix_pre, w_in, g_q, g_k, w_conv, b_conv, w_a_fwd, b_a_fwd, w_x_fwd, b_x_fwd, lam_fwd, w_a_bwd, b_a_bwd, w_x_bwd, b_x_bwd, lam_bwd, g_attn_out, g_rnn_out, w_out, g_mix_post, g_ffn2_pre, w_ffn2_gate, w_ffn2_up, w_ffn2_down, g_ffn2_post):
    stacked = (g_ffn1_pre, w_ffn1_gate, w_ffn1_up, w_ffn1_down, g_ffn1_post, g_mix_pre, w_in, g_q,
               g_k, w_conv, b_conv, w_a_fwd, b_a_fwd, w_x_fwd, b_x_fwd, lam_fwd, w_a_bwd, b_a_bwd,
               w_x_bwd, b_x_bwd, lam_bwd, g_attn_out, g_rnn_out, w_out, g_mix_post, g_ffn2_pre,
               w_ffn2_gate, w_ffn2_up, w_ffn2_down, g_ffn2_post)
    depth = g_ffn1_pre.shape[0]
    y_prompt, y_sample = x_prompt, x_sample
    rope = _rope_tables(max(x_prompt.shape[1], x_sample.shape[1]))
    for layer in range(depth):
        w = _prepare({n: a[layer] for n, a in zip(_PARAM_NAMES, stacked)})
        y_prompt = _layer(y_prompt, w, rope)
        y_sample = _layer(y_sample, w, rope)
    return (y_prompt, y_sample)
```
